```python
import jax, jax.numpy as jnp
from jax import lax
import numpy as np


D_MODEL = 1024
BATCH = 8
SEQ = 2048
DEPTH = 4

CHUNK = 64
D_CONV = D_MODEL // 2
D_ATTN = D_MODEL - D_CONV
N_HEADS = 8
HEAD_DIM = D_ATTN // N_HEADS
CONV_KERNEL = 31
LEFT_CHUNKS = 8
BAND_CHUNKS = LEFT_CHUNKS + 1
REL_CLIP = 128
N_EXPERTS = 32
TOP_K = 4
D_EXPERT = D_MODEL
SWIGLU_ALPHA = 1.702
SWIGLU_LIMIT = 7.0
MOE_BLOCK = 128
EPS = 1e-6
D_IN = 2 * D_CONV + 3 * D_ATTN

kernel_name = 'hybrid_conformer_chunkattn_moe_adaln'


def rms_norm(x, g):
    xf = x.astype(jnp.float32)
    y = xf * lax.rsqrt(jnp.mean(xf * xf, axis=-1, keepdims=True) + EPS)
    return (y * g).astype(x.dtype)


def layer_norm(x, g, b):
    xf = x.astype(jnp.float32)
    mu = jnp.mean(xf, axis=-1, keepdims=True)
    xc = xf - mu
    y = xc * lax.rsqrt(jnp.mean(xc * xc, axis=-1, keepdims=True) + EPS)
    return (y * g + b).astype(x.dtype)


def modulate(x, g, shift, scale):
    return rms_norm(x, g) * (1.0 + scale[:, None, :]) + shift[:, None, :]


def conv_mixer(a, gt, w, b, ln_g, ln_b):
    u = a * jax.nn.sigmoid(gt)
    y = lax.conv_general_dilated(
        u, w[:, None, :], window_strides=(1,), padding=((CONV_KERNEL - 1, 0),),
        dimension_numbers=('NWC', 'WIO', 'NWC'), feature_group_count=D_CONV) + b
    y = layer_norm(y, ln_g, ln_b)
    return jax.nn.silu(y)


def chunk_attention(q, k, v, q_g, k_g, rel_bias):
    B, S = q.shape[0], q.shape[1]
    nc = S // CHUNK
    shp = (B, nc, CHUNK, N_HEADS, HEAD_DIM)
    q = rms_norm(q.reshape(shp), q_g)
    k = rms_norm(k.reshape(shp), k_g)
    v = v.reshape(shp)
    pad = ((0, 0), (LEFT_CHUNKS, 0), (0, 0), (0, 0), (0, 0))
    band = np.arange(nc)[:, None] + np.arange(BAND_CHUNKS)[None, :]
    kb = jnp.pad(k, pad)[:, band].reshape(B, nc, BAND_CHUNKS * CHUNK, N_HEADS, HEAD_DIM)
    vb = jnp.pad(v, pad)[:, band].reshape(B, nc, BAND_CHUNKS * CHUNK, N_HEADS, HEAD_DIM)
    valid = np.repeat(band >= LEFT_CHUNKS, CHUNK, axis=1)
    dist = LEFT_CHUNKS * CHUNK + np.arange(CHUNK)[:, None] - np.arange(BAND_CHUNKS * CHUNK)[None, :]
    bias = rel_bias[:, np.clip(dist, -REL_CLIP, REL_CLIP) + REL_CLIP].astype(jnp.float32)
    s = jnp.einsum('bnqhd,bnkhd->bhnqk', q, kb, preferred_element_type=jnp.float32)
    s = s * (HEAD_DIM ** -0.5) + bias[None, :, None]
    s = jnp.where(valid[None, None, :, None, :], s, -jnp.inf)
    p = jax.nn.softmax(s, axis=-1).astype(v.dtype)
    o = jnp.einsum('bhnqk,bnkhd->bnqhd', p, vb)
    return o.reshape(B, S, D_ATTN)


def moe(h, router_w, router_b, w1, b1, w2, b2):
    B, S, D = h.shape
    T = B * S
    tk = T * TOP_K
    hf = h.reshape(T, D)
    logits = jnp.matmul(hf, router_w, preferred_element_type=jnp.float32) + router_b.astype(jnp.float32)
    top_val, top_idx = lax.top_k(logits, TOP_K)
    gates = jax.nn.softmax(top_val, axis=-1)
    e = top_idx.reshape(-1).astype(jnp.int32)
    tok = jnp.arange(tk, dtype=jnp.int32) // TOP_K
    gw = gates.reshape(-1)
    order = jnp.argsort(e)
    e_s = e[order]
    counts = jnp.bincount(e, length=N_EXPERTS).astype(jnp.int32)
    starts = jnp.cumsum(counts) - counts
    pad_counts = (counts + MOE_BLOCK - 1) // MOE_BLOCK * MOE_BLOCK
    pad_ends = jnp.cumsum(pad_counts)
    pad_starts = pad_ends - pad_counts
    dest = pad_starts[e_s] + (jnp.arange(tk, dtype=jnp.int32) - starts[e_s])
    n_slots = -(-tk // MOE_BLOCK) * MOE_BLOCK + N_EXPERTS * MOE_BLOCK
    n_blocks = n_slots // MOE_BLOCK
    slot_tok = jnp.full((n_slots,), T, jnp.int32).at[dest].set(tok[order])
    slot_w = jnp.zeros((n_slots,), jnp.float32).at[dest].set(gw[order])
    block_exp = jnp.minimum(
        jnp.searchsorted(pad_ends, jnp.arange(n_blocks, dtype=jnp.int32) * MOE_BLOCK, side='right'),
        N_EXPERTS - 1)
    h_pad = jnp.concatenate([hf, jnp.zeros((1, D), hf.dtype)], axis=0)
    xb = h_pad[slot_tok].reshape(n_blocks, MOE_BLOCK, D)

    def expert_block(args):
        xs, ei = args
        u = xs @ w1[ei] + b1[ei]
        glu, lin = u[:, :D_EXPERT], u[:, D_EXPERT:]
        glu = jnp.minimum(glu, SWIGLU_LIMIT)
        lin = jnp.clip(lin, -SWIGLU_LIMIT, SWIGLU_LIMIT)
        act = glu * jax.nn.sigmoid(SWIGLU_ALPHA * glu) * (lin + 1.0)
        return act @ w2[ei] + b2[ei]

    yb = lax.map(expert_block, (xb, block_exp)).reshape(n_slots, D)
    y = jnp.zeros((T + 1, D), h.dtype).at[slot_tok].add(yb * slot_w[:, None].astype(h.dtype))
    return y[:T].reshape(B, S, D)


def setup_inputs(seed: int = 0) -> dict:
    key = jax.random.key(seed)
    ks = jax.random.split(key, 24)

    def nrm(k, shape, scale):
        return scale * jax.random.normal(k, shape, jnp.float32)

    d = D_MODEL
    return {
        'x': nrm(ks[0], (BATCH, SEQ, d), 1.0),
        'c': nrm(ks[1], (BATCH, d), 1.0),
        'ada_w': nrm(ks[2], (DEPTH, d, 6 * d), 0.5 * d ** -0.5),
        'ada_b': nrm(ks[3], (DEPTH, 6 * d), 0.02),
        'norm1_g': 1.0 + nrm(ks[4], (DEPTH, d), 0.02),
        'w_in': nrm(ks[5], (DEPTH, d, D_IN), d ** -0.5),
        'conv_w': nrm(ks[6], (DEPTH, CONV_KERNEL, D_CONV), CONV_KERNEL ** -0.5),
        'conv_b': nrm(ks[7], (DEPTH, D_CONV), 0.02),
        'conv_ln_g': 1.0 + nrm(ks[8], (DEPTH, D_CONV), 0.02),
        'conv_ln_b': nrm(ks[9], (DEPTH, D_CONV), 0.02),
        'q_norm_g': 1.0 + nrm(ks[10], (DEPTH, HEAD_DIM), 0.02),
        'k_norm_g': 1.0 + nrm(ks[11], (DEPTH, HEAD_DIM), 0.02),
        'rel_bias': nrm(ks[12], (DEPTH, N_HEADS, 2 * REL_CLIP + 1), 0.1),
        'w_out': nrm(ks[13], (DEPTH, d, d), d ** -0.5),
        'norm2_g': 1.0 + nrm(ks[14], (DEPTH, d), 0.02),
        'router_w': nrm(ks[15], (DEPTH, d, N_EXPERTS), d ** -0.5),
        'router_b': nrm(ks[16], (DEPTH, N_EXPERTS), 0.01),
        'exp_w1': nrm(ks[17], (DEPTH, N_EXPERTS, d, 2 * D_EXPERT), d ** -0.5),
        'exp_b1': nrm(ks[18], (DEPTH, N_EXPERTS, 2 * D_EXPERT), 0.02),
        'exp_w2': nrm(ks[19], (DEPTH, N_EXPERTS, D_EXPERT, d), D_EXPERT ** -0.5),
        'exp_b2': nrm(ks[20], (DEPTH, N_EXPERTS, d), 0.02),
    }


def reference(x, c, ada_w, ada_b, norm1_g, w_in, conv_w, conv_b, conv_ln_g, conv_ln_b,
              q_norm_g, k_norm_g, rel_bias, w_out, norm2_g, router_w, router_b,
              exp_w1, exp_b1, exp_w2, exp_b2):
    c_act = jax.nn.silu(c)
    split_pts = [D_CONV, 2 * D_CONV, 2 * D_CONV + D_ATTN, 2 * D_CONV + 2 * D_ATTN]
    for l in range(DEPTH):
        mod = c_act @ ada_w[l] + ada_b[l]
        sh1, sc1, g1, sh2, sc2, g2 = jnp.split(mod, 6, axis=-1)
        h = modulate(x, norm1_g[l], sh1, sc1)
        u = h @ w_in[l]
        a, gt, q, k, v = jnp.split(u, split_pts, axis=-1)
        y_conv = conv_mixer(a, gt, conv_w[l], conv_b[l], conv_ln_g[l], conv_ln_b[l])
        y_attn = chunk_attention(q, k, v, q_norm_g[l], k_norm_g[l], rel_bias[l])
        mixed = jnp.concatenate([y_conv, y_attn], axis=-1) @ w_out[l]
        x = x + g1[:, None, :] * mixed
        h2 = modulate(x, norm2_g[l], sh2, sc2)
        y_moe = moe(h2, router_w[l], router_b[l], exp_w1[l], exp_b1[l], exp_w2[l], exp_b2[l])
        x = x + g2[:, None, :] * y_moe
    return x
```

```python
import functools

import jax
import jax.numpy as jnp
import numpy as np
from jax import lax
from jax.experimental import pallas as pl
from jax.experimental.pallas import tpu as pltpu

D_MODEL = 1024
BATCH = 8
SEQ = 2048
DEPTH = 4
TOKENS = BATCH * SEQ

CHUNK = 64
D_CONV = D_MODEL // 2
D_ATTN = D_MODEL - D_CONV
N_HEADS = 8
HEAD_DIM = D_ATTN // N_HEADS
CONV_KERNEL = 31
LEFT_CHUNKS = 8
REL_CLIP = 128
N_EXPERTS = 32
TOP_K = 4
D_EXPERT = D_MODEL
SWIGLU_ALPHA = 1.702
SWIGLU_LIMIT = 7.0
EPS = 1e-6
D_IN = 2 * D_CONV + 3 * D_ATTN

F32 = jnp.float32
BF16 = jnp.bfloat16
NEG_INF = float("-inf")

TM = 512
TILES_PER_SEQ = SEQ // TM
Q_PAIR = 2 * CHUNK
KEY_WIN = (LEFT_CHUNKS + 2) * CHUNK
PAIRS_PER_TILE = TM // Q_PAIR
HALO = 32
CONV_ROWS = 32
MOE_M = 256
N_SLOTS = TOKENS * TOP_K + N_EXPERTS * MOE_M
N_BLOCKS = N_SLOTS // MOE_M
TMC = 256
DMA_UNROLL = 8
LANES = 128
SUBLANES = 8

VMEM_LIMIT = 56 * 1024 * 1024


def _sigmoid(x):
    return 1.0 / (1.0 + jnp.exp(-x))


def _ada_kernel(c_ref, w_ref, b_ref, o_ref):
    c = c_ref[...]
    ca = (c * _sigmoid(c)).astype(BF16)
    o_ref[0] = jnp.dot(ca, w_ref[0].astype(BF16), preferred_element_type=F32) + b_ref[0]


def _ada_mod(c, ada_w, ada_b):
    n_tile = 1536
    return pl.pallas_call(
        _ada_kernel,
        grid=(DEPTH, 6 * D_MODEL // n_tile),
        in_specs=[
            pl.BlockSpec((BATCH, D_MODEL), lambda l, n: (0, 0)),
            pl.BlockSpec((1, D_MODEL, n_tile), lambda l, n: (l, 0, n)),
            pl.BlockSpec((1, 1, n_tile), lambda l, n: (l, 0, n)),
        ],
        out_specs=pl.BlockSpec((1, BATCH, n_tile), lambda l, n: (l, 0, n)),
        out_shape=jax.ShapeDtypeStruct((DEPTH, BATCH, 6 * D_MODEL), F32),
        compiler_params=pltpu.CompilerParams(
            dimension_semantics=("arbitrary", "arbitrary"), vmem_limit_bytes=VMEM_LIMIT),
        name="ada_mod",
    )(c, ada_w, ada_b.reshape(DEPTH, 1, 6 * D_MODEL))


def _modulated_norm(x, g, sc, sh):
    ms = jnp.mean(x * x, axis=-1, keepdims=True)
    return x * lax.rsqrt(ms + EPS) * g * (1.0 + sc) + sh


def _inproj_kernel(x_ref, sh_ref, sc_ref, g_ref, w_ref, seg_ref, qg_ref, kg_ref,
                   ug_ref, q_ref, k_ref, v_ref):
    h = _modulated_norm(x_ref[...], g_ref[...], sc_ref[0], sh_ref[0])
    u = jnp.dot(h.astype(BF16), w_ref[...], preferred_element_type=F32)
    a = u[:, :D_CONV]
    gt = u[:, D_CONV:2 * D_CONV]
    ug_ref[...] = (a * _sigmoid(gt)).astype(BF16)
    o = 2 * D_CONV
    q = u[:, o:o + D_ATTN]
    k = u[:, o + D_ATTN:o + 2 * D_ATTN]
    v = u[:, o + 2 * D_ATTN:]
    qms = jnp.dot((q * q).astype(BF16), seg_ref[...], preferred_element_type=F32)
    kms = jnp.dot((k * k).astype(BF16), seg_ref[...], preferred_element_type=F32)
    q_ref[...] = (q * lax.rsqrt(qms + EPS) * qg_ref[...]).astype(BF16)
    k_ref[...] = (k * lax.rsqrt(kms + EPS) * kg_ref[...]).astype(BF16)
    v_ref[...] = v.astype(BF16)


def _inproj(x, mod3, norm_g, w_in_bf, seg, qg, kg):
    row = lambda i: (i, 0)
    const = lambda i: (0, 0)
    out_sd = jax.ShapeDtypeStruct((TOKENS, D_CONV), BF16)
    return pl.pallas_call(
        _inproj_kernel,
        grid=(TOKENS // TM,),
        in_specs=[
            pl.BlockSpec((TM, D_MODEL), row),
            pl.BlockSpec((1, 1, D_MODEL), lambda i: (i // TILES_PER_SEQ, 0, 0)),
            pl.BlockSpec((1, 1, D_MODEL), lambda i: (i // TILES_PER_SEQ, 0, 1)),
            pl.BlockSpec((1, D_MODEL), const),
            pl.BlockSpec((D_MODEL, D_IN), const),
            pl.BlockSpec((D_ATTN, D_ATTN), const),
            pl.BlockSpec((1, D_ATTN), const),
            pl.BlockSpec((1, D_ATTN), const),
        ],
        out_specs=[pl.BlockSpec((TM, D_CONV), row)] * 4,
        out_shape=[out_sd] * 4,
        compiler_params=pltpu.CompilerParams(
            dimension_semantics=("arbitrary",), vmem_limit_bytes=VMEM_LIMIT),
        name="inproj",
    )(x, mod3, mod3, norm_g, w_in_bf, seg, qg, kg)


def _conv_kernel(cur_ref, halo_ref, w_ref, b_ref, lg_ref, lb_ref, o_ref, xe_ref):
    i = pl.program_id(0)
    seq_start = (i % TILES_PER_SEQ) == 0
    halo = halo_ref[...].astype(F32)
    xe_ref[0, 0:HALO, :] = jnp.where(seq_start, 0.0, halo)
    xe_ref[0, HALO:, :] = cur_ref[...].astype(F32)
    ext = HALO + TM
    for s in range(1, SUBLANES):
        xe_ref[s, 0:ext - SUBLANES, :] = xe_ref[0, s:ext - SUBLANES + s, :]
    n_lane = D_CONV // LANES
    first_tap = HALO - (CONV_KERNEL - 1)

    def body(rc, carry):
        r0 = pl.multiple_of(rc * CONV_ROWS, CONV_ROWS)
        accs = []
        for c in range(n_lane):
            cs = slice(c * LANES, (c + 1) * LANES)
            acc = jnp.zeros((CONV_ROWS, LANES), F32) + b_ref[:, cs]
            for j in range(CONV_KERNEL):
                shift = (first_tap + j) % SUBLANES
                aligned = first_tap + j - shift
                acc = acc + w_ref[j:j + 1, cs] * xe_ref[shift, pl.ds(r0 + aligned, CONV_ROWS), cs]
            accs.append(acc)
        tot = accs[0].sum(axis=-1, keepdims=True)
        for c in range(1, n_lane):
            tot = tot + accs[c].sum(axis=-1, keepdims=True)
        mu = tot * (1.0 / D_CONV)
        cen = [a - mu for a in accs]
        var = (cen[0] * cen[0]).sum(axis=-1, keepdims=True)
        for c in range(1, n_lane):
            var = var + (cen[c] * cen[c]).sum(axis=-1, keepdims=True)
        inv = lax.rsqrt(var * (1.0 / D_CONV) + EPS)
        for c in range(n_lane):
            cs = slice(c * LANES, (c + 1) * LANES)
            y = cen[c] * inv * lg_ref[:, cs] + lb_ref[:, cs]
            o_ref[pl.ds(r0, CONV_ROWS), cs] = (y * _sigmoid(y)).astype(BF16)
        return carry

    lax.fori_loop(0, TM // CONV_ROWS, body, 0)


def _conv(ug, conv_w, conv_b, ln_g, ln_b):
    const = lambda i: (0, 0)
    halo_per_tile = TM // HALO
    return pl.pallas_call(
        _conv_kernel,
        grid=(TOKENS // TM,),
        in_specs=[
            pl.BlockSpec((TM, D_CONV), lambda i: (i, 0)),
            pl.BlockSpec((HALO, D_CONV), lambda i: (jnp.maximum(i * halo_per_tile - 1, 0), 0)),
            pl.BlockSpec((CONV_KERNEL, D_CONV), const),
            pl.BlockSpec((1, D_CONV), const),
            pl.BlockSpec((1, D_CONV), const),
            pl.BlockSpec((1, D_CONV), const),
        ],
        out_specs=pl.BlockSpec((TM, D_CONV), lambda i: (i, 0)),
        out_shape=jax.ShapeDtypeStruct((TOKENS, D_CONV), BF16),
        scratch_shapes=[pltpu.VMEM((SUBLANES, HALO + TM, D_CONV), F32)],
        compiler_params=pltpu.CompilerParams(
            dimension_semantics=("arbitrary",), vmem_limit_bytes=VMEM_LIMIT),
        name="conv_mixer",
    )(ug, ug, conv_w, conv_b, ln_g, ln_b)


def _attn_kernel(q_ref, kp_ref, kc_ref, vp_ref, vc_ref, bias_ref, o_ref, kw_ref, vw_ref):
    i = pl.program_id(0)
    first_valid = jnp.where((i % TILES_PER_SEQ) == 0, TM, 0)
    kw_ref[0:TM, :] = kp_ref[...]
    kw_ref[TM:, :] = kc_ref[...]
    vw_ref[0:TM, :] = vp_ref[...]
    vw_ref[TM:, :] = vc_ref[...]
    col = lax.broadcasted_iota(jnp.int32, (Q_PAIR, KEY_WIN), 1)

    def body(cp, carry):
        r0 = pl.multiple_of(cp * Q_PAIR, Q_PAIR)
        valid = (col + r0) >= first_valid
        for h in range(N_HEADS):
            hs = slice(h * HEAD_DIM, (h + 1) * HEAD_DIM)
            qh = q_ref[pl.ds(r0, Q_PAIR), hs]
            kb = kw_ref[pl.ds(r0, KEY_WIN), hs]
            vb = vw_ref[pl.ds(r0, KEY_WIN), hs]
            s = lax.dot_general(qh, kb, (((1,), (1,)), ((), ())), preferred_element_type=F32)
            s = jnp.where(valid, s + bias_ref[h], NEG_INF)
            m = jnp.max(s, axis=-1, keepdims=True)
            e = jnp.exp(s - m)
            l = jnp.sum(e, axis=-1, keepdims=True)
            o = jnp.dot(e.astype(BF16), vb, preferred_element_type=F32) * (1.0 / l)
            o_ref[pl.ds(r0, Q_PAIR), hs] = o.astype(BF16)
        return carry

    lax.fori_loop(0, PAIRS_PER_TILE, body, 0)


def _attention(q, k, v, bias2):
    cur = lambda i: (i, 0)
    prev = lambda i: (jnp.maximum(i - 1, 0), 0)
    blk = (TM, D_ATTN)
    return pl.pallas_call(
        _attn_kernel,
        grid=(TOKENS // TM,),
        in_specs=[
            pl.BlockSpec(blk, cur),
            pl.BlockSpec(blk, prev), pl.BlockSpec(blk, cur),
            pl.BlockSpec(blk, prev), pl.BlockSpec(blk, cur),
            pl.BlockSpec((N_HEADS, Q_PAIR, KEY_WIN), lambda i: (0, 0, 0)),
        ],
        out_specs=pl.BlockSpec(blk, cur),
        out_shape=jax.ShapeDtypeStruct((TOKENS, D_ATTN), BF16),
        scratch_shapes=[pltpu.VMEM((2 * TM, D_ATTN), BF16), pltpu.VMEM((2 * TM, D_ATTN), BF16)],
        compiler_params=pltpu.CompilerParams(
            dimension_semantics=("arbitrary",), vmem_limit_bytes=VMEM_LIMIT),
        name="chunk_attention",
    )(q, k, k, v, v, bias2)


def _outproj_kernel(yc_ref, ya_ref, x_ref, g1_ref, wt_ref, wb_ref, sh_ref, sc_ref, g_ref,
                    rwh_ref, rwl_ref, rb_ref, x1_ref, h2_ref, idx_ref, gate_ref):
    mixed = (jnp.dot(yc_ref[...], wt_ref[...], preferred_element_type=F32)
             + jnp.dot(ya_ref[...], wb_ref[...], preferred_element_type=F32))
    x1 = x_ref[...] + g1_ref[0] * mixed
    x1_ref[...] = x1
    h2 = _modulated_norm(x1, g_ref[...], sc_ref[0], sh_ref[0])
    h2_ref[...] = h2
    hi = h2.astype(BF16)
    lo = (h2 - hi.astype(F32)).astype(BF16)
    logits = (jnp.dot(hi, rwh_ref[...], preferred_element_type=F32)
              + jnp.dot(hi, rwl_ref[...], preferred_element_type=F32)
              + jnp.dot(lo, rwh_ref[...], preferred_element_type=F32)) + rb_ref[...]
    lane = lax.broadcasted_iota(jnp.int32, logits.shape, 1)
    out_lane = lax.broadcasted_iota(jnp.int32, (TM, LANES), 1)
    idx_out = jnp.zeros((TM, LANES), jnp.int32)
    val_out = jnp.zeros((TM, LANES), F32)
    top0 = None
    denom = None
    for kk in range(TOP_K):
        m = jnp.max(logits, axis=-1, keepdims=True)
        am = jnp.min(jnp.where(logits == m, lane, N_EXPERTS), axis=-1, keepdims=True)
        logits = jnp.where(lane == am, NEG_INF, logits)
        if kk == 0:
            top0 = m
        e = jnp.exp(m - top0)
        denom = e if kk == 0 else denom + e
        idx_out = jnp.where(out_lane == kk, am, idx_out)
        val_out = jnp.where(out_lane == kk, e, val_out)
    idx_ref[...] = idx_out
    gate_ref[...] = val_out * (1.0 / denom)


def _outproj(yc, ya, x, mod3, w_top, w_bot, norm_g, rw_hi, rw_lo, rb):
    row = lambda i: (i, 0)
    const = lambda i: (0, 0)
    modspec = lambda j: pl.BlockSpec((1, 1, D_MODEL), lambda i: (i // TILES_PER_SEQ, 0, j))
    return pl.pallas_call(
        _outproj_kernel,
        grid=(TOKENS // TM,),
        in_specs=[
            pl.BlockSpec((TM, D_CONV), row),
            pl.BlockSpec((TM, D_ATTN), row),
            pl.BlockSpec((TM, D_MODEL), row),
            modspec(2),
            pl.BlockSpec((D_CONV, D_MODEL), const),
            pl.BlockSpec((D_ATTN, D_MODEL), const),
            modspec(3),
            modspec(4),
            pl.BlockSpec((1, D_MODEL), const),
            pl.BlockSpec((D_MODEL, N_EXPERTS), const),
            pl.BlockSpec((D_MODEL, N_EXPERTS), const),
            pl.BlockSpec((1, N_EXPERTS), const),
        ],
        out_specs=[
            pl.BlockSpec((TM, D_MODEL), row),
            pl.BlockSpec((TM, D_MODEL), row),
            pl.BlockSpec((TM, LANES), row),
            pl.BlockSpec((TM, LANES), row),
        ],
        out_shape=[
            jax.ShapeDtypeStruct((TOKENS, D_MODEL), F32),
            jax.ShapeDtypeStruct((TOKENS, D_MODEL), F32),
            jax.ShapeDtypeStruct((TOKENS, LANES), jnp.int32),
            jax.ShapeDtypeStruct((TOKENS, LANES), F32),
        ],
        compiler_params=pltpu.CompilerParams(
            dimension_semantics=("arbitrary",), vmem_limit_bytes=VMEM_LIMIT),
        name="outproj_router",
    )(yc, ya, x, mod3, w_top, w_bot, mod3, mod3, norm_g, rw_hi, rw_lo, rb)


def _row_copy(src_hbm, row, dst_ref, dst_row, sem):
    return pltpu.make_async_copy(
        src_hbm.at[pl.ds(row, 1), :], dst_ref.at[pl.ds(dst_row, 1), :], sem)


def _start_row_gather(idx_ref, src_hbm, dst_ref, dst_base, sem, n_rows):
    def body(r, carry):
        base = r * DMA_UNROLL
        for u in range(DMA_UNROLL):
            _row_copy(src_hbm, idx_ref[0, 0, base + u], dst_ref, dst_base + base + u, sem).start()
        return carry

    lax.fori_loop(0, n_rows // DMA_UNROLL, body, 0)


def _wait_row_gather(src_hbm, dst_ref, dst_base, sem, n_rows):
    pltpu.make_async_copy(
        src_hbm.at[pl.ds(0, n_rows), :], dst_ref.at[pl.ds(dst_base, n_rows), :], sem).wait()


def _expert_kernel(be_ref, nu_ref, idx0_ref, idxn_ref, h_hbm, w1_ref, b1_ref, w2_ref, b2_ref,
                   o_ref, xbuf, w1b, w2b, sem):
    i = pl.program_id(0)
    n_used = nu_ref[0]
    slot = i % 2
    nslot = (i + 1) % 2

    @pl.when(jnp.logical_and(i == 0, n_used > 0))
    def _():
        _start_row_gather(idx0_ref, h_hbm, xbuf, 0, sem.at[0], MOE_M)

    @pl.when(i + 1 < n_used)
    def _():
        _start_row_gather(idxn_ref, h_hbm, xbuf, pl.multiple_of(nslot * MOE_M, MOE_M),
                          sem.at[nslot], MOE_M)

    @pl.when(i < n_used)
    def _():
        prev_e = be_ref[jnp.maximum(i - 1, 0)]

        @pl.when(jnp.logical_or(i == 0, be_ref[i] != prev_e))
        def _():
            w1b[...] = w1_ref[0, 0].astype(BF16)
            w2b[...] = w2_ref[0, 0].astype(BF16)

        base = pl.multiple_of(slot * MOE_M, MOE_M)
        _wait_row_gather(h_hbm, xbuf, base, sem.at[slot], MOE_M)
        xs = xbuf[pl.ds(base, MOE_M), :].astype(BF16)
        u = jnp.dot(xs, w1b[...], preferred_element_type=F32) + b1_ref[0, 0]
        glu = jnp.minimum(u[:, :D_EXPERT], SWIGLU_LIMIT)
        lin = jnp.clip(u[:, D_EXPERT:], -SWIGLU_LIMIT, SWIGLU_LIMIT)
        act = glu * _sigmoid(SWIGLU_ALPHA * glu) * (lin + 1.0)
        o_ref[...] = jnp.dot(act.astype(BF16), w2b[...], preferred_element_type=F32) + b2_ref[0, 0]

    @pl.when(i >= n_used)
    def _():
        o_ref[...] = jnp.zeros_like(o_ref)


def _experts(layer, block_exp, n_used, slot_tok3, h2, w1, b1, w2, b2):
    wmap = lambda i, be, nu: (layer, be[i], 0, 0)
    grid_spec = pltpu.PrefetchScalarGridSpec(
        num_scalar_prefetch=2,
        grid=(N_BLOCKS,),
        in_specs=[
            pl.BlockSpec((1, 1, MOE_M), lambda i, be, nu: (0, 0, 0), memory_space=pltpu.SMEM),
            pl.BlockSpec((1, 1, MOE_M), lambda i, be, nu: (jnp.minimum(i + 1, N_BLOCKS - 1), 0, 0),
                         memory_space=pltpu.SMEM),
            pl.BlockSpec(memory_space=pl.ANY),
            pl.BlockSpec((1, 1, D_MODEL, 2 * D_EXPERT), wmap),
            pl.BlockSpec((1, 1, 1, 2 * D_EXPERT), wmap),
            pl.BlockSpec((1, 1, D_EXPERT, D_MODEL), wmap),
            pl.BlockSpec((1, 1, 1, D_MODEL), wmap),
        ],
        out_specs=pl.BlockSpec((MOE_M, D_MODEL), lambda i, be, nu: (i, 0)),
        scratch_shapes=[
            pltpu.VMEM((2 * MOE_M, D_MODEL), F32),
            pltpu.VMEM((D_MODEL, 2 * D_EXPERT), BF16),
            pltpu.VMEM((D_EXPERT, D_MODEL), BF16),
            pltpu.SemaphoreType.DMA((2,)),
        ],
    )
    return pl.pallas_call(
        _expert_kernel,
        grid_spec=grid_spec,
        out_shape=jax.ShapeDtypeStruct((N_SLOTS, D_MODEL), F32),
        compiler_params=pltpu.CompilerParams(
            dimension_semantics=("arbitrary",), vmem_limit_bytes=VMEM_LIMIT),
        name="experts",
    )(block_exp, n_used, slot_tok3, slot_tok3, h2,
      w1, b1.reshape(DEPTH, N_EXPERTS, 1, 2 * D_EXPERT), w2,
      b2.reshape(DEPTH, N_EXPERTS, 1, D_MODEL))


def _combine_kernel(idx0_ref, idxn_ref, ys_hbm, x1_ref, g2_ref, gate_ref, o_ref, buf, sem):
    i = pl.program_id(0)
    n = pl.num_programs(0)
    rows = TMC * TOP_K
    slot = i % 2
    nslot = (i + 1) % 2

    @pl.when(i == 0)
    def _():
        _start_row_gather(idx0_ref, ys_hbm, buf, 0, sem.at[0], rows)

    @pl.when(i + 1 < n)
    def _():
        _start_row_gather(idxn_ref, ys_hbm, buf, pl.multiple_of(nslot * rows, rows),
                          sem.at[nslot], rows)

    base = pl.multiple_of(slot * rows, rows)
    _wait_row_gather(ys_hbm, buf, base, sem.at[slot], rows)
    gates = gate_ref[...]
    acc = jnp.zeros((TMC, D_MODEL), F32)
    for kk in range(TOP_K):
        acc = acc + gates[:, kk:kk + 1] * buf[pl.ds(base + kk * TMC, TMC), :]
    o_ref[...] = x1_ref[...] + g2_ref[0] * acc


def _combine(dest3, ys, x1, mod3, gates):
    n_tiles = TOKENS // TMC
    tiles_per_seq = SEQ // TMC
    rows = TMC * TOP_K
    return pl.pallas_call(
        _combine_kernel,
        grid=(n_tiles,),
        in_specs=[
            pl.BlockSpec((1, 1, rows), lambda i: (0, 0, 0), memory_space=pltpu.SMEM),
            pl.BlockSpec((1, 1, rows), lambda i: (jnp.minimum(i + 1, n_tiles - 1), 0, 0),
                         memory_space=pltpu.SMEM),
            pl.BlockSpec(memory_space=pl.ANY),
            pl.BlockSpec((TMC, D_MODEL), lambda i: (i, 0)),
            pl.BlockSpec((1, 1, D_MODEL), lambda i: (i // tiles_per_seq, 0, 5)),
            pl.BlockSpec((TMC, LANES), lambda i: (i, 0)),
        ],
        out_specs=pl.BlockSpec((TMC, D_MODEL), lambda i: (i, 0)),
        out_shape=jax.ShapeDtypeStruct((TOKENS, D_MODEL), F32),
        scratch_shapes=[
            pltpu.VMEM((2 * rows, D_MODEL), F32),
            pltpu.SemaphoreType.DMA((2,)),
        ],
        compiler_params=pltpu.CompilerParams(
            dimension_semantics=("arbitrary",), vmem_limit_bytes=VMEM_LIMIT),
        name="moe_combine",
    )(dest3, dest3, ys, x1, mod3, gates)


def _dispatch_tables(top_idx):
    e = top_idx
    onehot = (e[:, :, None] == jnp.arange(N_EXPERTS, dtype=jnp.int32)).astype(jnp.int32).sum(axis=1)
    incl = jnp.cumsum(onehot, axis=0)
    rank = jnp.take_along_axis(incl - onehot, e, axis=1)
    counts = incl[-1]
    pad_counts = (counts + MOE_M - 1) // MOE_M * MOE_M
    pad_ends = jnp.cumsum(pad_counts)
    pad_starts = pad_ends - pad_counts
    dest = pad_starts[e] + rank
    tok = jnp.broadcast_to(jnp.arange(TOKENS, dtype=jnp.int32)[:, None], (TOKENS, TOP_K))
    slot_tok = jnp.zeros((N_SLOTS,), jnp.int32).at[dest.reshape(-1)].set(tok.reshape(-1))
    block_exp = jnp.minimum(
        jnp.searchsorted(pad_ends, jnp.arange(N_BLOCKS, dtype=jnp.int32) * MOE_M, side="right"),
        N_EXPERTS - 1).astype(jnp.int32)
    n_used = (pad_ends[-1] // MOE_M).astype(jnp.int32).reshape(1)
    return dest.astype(jnp.int32), slot_tok, block_exp, n_used


_I = np.arange(Q_PAIR)[:, None]
_K = np.arange(KEY_WIN)[None, :]
_REL_IDX = np.clip(_I + LEFT_CHUNKS * CHUNK - _K, -REL_CLIP, REL_CLIP) + REL_CLIP
_BAND_OK = (_K // CHUNK >= _I // CHUNK) & (_K // CHUNK <= _I // CHUNK + LEFT_CHUNKS)
_SEG = np.kron(np.eye(N_HEADS), np.full((HEAD_DIM, HEAD_DIM), 1.0 / HEAD_DIM)).astype(np.float32)


def kernel(x, c, ada_w, ada_b, norm1_g, w_in, conv_w, conv_b, conv_ln_g, conv_ln_b, q_norm_g,
           k_norm_g, rel_bias, w_out, norm2_g, router_w, router_b, exp_w1, exp_b1, exp_w2, exp_b2):
    xf = x.reshape(TOKENS, D_MODEL)
    mod = _ada_mod(c, ada_w, ada_b)
    seg = jnp.asarray(_SEG, BF16)
    for l in range(DEPTH):
        mod3 = mod[l].reshape(BATCH, 1, 6 * D_MODEL)
        qg = (jnp.tile(q_norm_g[l], N_HEADS) * (HEAD_DIM ** -0.5)).reshape(1, D_ATTN)
        kg = jnp.tile(k_norm_g[l], N_HEADS).reshape(1, D_ATTN)
        ug, q, k, v = _inproj(xf, mod3, norm1_g[l].reshape(1, D_MODEL), w_in[l].astype(BF16),
                              seg, qg, kg)
        y_conv = _conv(ug, conv_w[l], conv_b[l].reshape(1, D_CONV),
                       conv_ln_g[l].reshape(1, D_CONV), conv_ln_b[l].reshape(1, D_CONV))
        bias2 = jnp.where(_BAND_OK[None], rel_bias[l][:, _REL_IDX], NEG_INF).astype(F32)
        y_attn = _attention(q, k, v, bias2)
        w_out_bf = w_out[l].astype(BF16)
        rw_hi = router_w[l].astype(BF16)
        rw_lo = (router_w[l] - rw_hi.astype(F32)).astype(BF16)
        x1, h2, idx_pad, gate_pad = _outproj(
            y_conv, y_attn, xf, mod3, w_out_bf[:D_CONV], w_out_bf[D_CONV:],
            norm2_g[l].reshape(1, D_MODEL), rw_hi, rw_lo, router_b[l].reshape(1, N_EXPERTS))
        dest, slot_tok, block_exp, n_used = _dispatch_tables(idx_pad[:, :TOP_K])
        ys = _experts(l, block_exp, n_used, slot_tok.reshape(N_BLOCKS, 1, MOE_M), h2,
                      exp_w1, exp_b1, exp_w2, exp_b2)
        dest3 = dest.reshape(TOKENS // TMC, TMC, TOP_K).transpose(0, 2, 1).reshape(
            TOKENS // TMC, 1, TMC * TOP_K)
        xf = _combine(dest3, ys, x1, mod3, gate_pad)
    return xf.reshape(BATCH, SEQ, D_MODEL)
```

```python
import jax
import jax.numpy as jnp
import numpy as np
from jax import lax
from jax.experimental import pallas as pl
from jax.experimental.pallas import tpu as pltpu

D_MODEL = 1024
BATCH = 8
SEQ = 2048
DEPTH = 4
TOKENS = BATCH * SEQ

CHUNK = 64
D_CONV = D_MODEL // 2
D_ATTN = D_MODEL - D_CONV
N_HEADS = 8
HEAD_DIM = D_ATTN // N_HEADS
CONV_KERNEL = 31
LEFT_CHUNKS = 8
REL_CLIP = 128
N_EXPERTS = 32
TOP_K = 4
D_EXPERT = D_MODEL
SWIGLU_ALPHA = 1.702
SWIGLU_LIMIT = 7.0
EPS = 1e-6
D_IN = 2 * D_CONV + 3 * D_ATTN

F32 = jnp.float32
BF16 = jnp.bfloat16
I32 = jnp.int32
NEG_INF = float("-inf")

LANES = 128
SUBLANES = 8
TM = 512
TILES_PER_SEQ = SEQ // TM
Q_PAIR = 2 * CHUNK
KEY_WIN = (LEFT_CHUNKS + 2) * CHUNK
PAIRS_PER_TILE = TM // Q_PAIR
HALO = 32
CONV_ROWS = 32
MOE_M = 256
EXPERT_CAP = TOKENS
CAP_BLOCKS = EXPERT_CAP // MOE_M
N_BLOCKS = TOKENS * TOP_K // MOE_M + N_EXPERTS
ROW_TILE = D_MODEL // LANES
TMC = 256
DMA_UNROLL = 8

VMEM_LIMIT = 56 * 1024 * 1024


def _sigmoid(x):
    return 1.0 / (1.0 + jnp.exp(-x))


def _ada_kernel(c_ref, w_ref, b_ref, o_ref):
    c = c_ref[...]
    ca = (c * _sigmoid(c)).astype(BF16)
    o_ref[0] = jnp.dot(ca, w_ref[0].astype(BF16), preferred_element_type=F32) + b_ref[0]


def _ada_mod(c, ada_w, ada_b):
    n_tile = 1536
    return pl.pallas_call(
        _ada_kernel,
        grid=(DEPTH, 6 * D_MODEL // n_tile),
        in_specs=[
            pl.BlockSpec((BATCH, D_MODEL), lambda l, n: (0, 0)),
            pl.BlockSpec((1, D_MODEL, n_tile), lambda l, n: (l, 0, n)),
            pl.BlockSpec((1, 1, n_tile), lambda l, n: (l, 0, n)),
        ],
        out_specs=pl.BlockSpec((1, BATCH, n_tile), lambda l, n: (l, 0, n)),
        out_shape=jax.ShapeDtypeStruct((DEPTH, BATCH, 6 * D_MODEL), F32),
        compiler_params=pltpu.CompilerParams(
            dimension_semantics=("arbitrary", "arbitrary"), vmem_limit_bytes=VMEM_LIMIT),
        name="ada_mod",
    )(c, ada_w, ada_b.reshape(DEPTH, 1, 6 * D_MODEL))


def _modulated_norm(x, g, sc, sh):
    ms = jnp.mean(x * x, axis=-1, keepdims=True)
    return x * lax.rsqrt(ms + EPS) * g * (1.0 + sc) + sh


def _inproj_kernel(x_ref, sh_ref, sc_ref, g_ref, w_ref, seg_ref, qg_ref, kg_ref,
                   ug_ref, q_ref, k_ref, v_ref):
    h = _modulated_norm(x_ref[...], g_ref[...], sc_ref[0], sh_ref[0])
    u = jnp.dot(h.astype(BF16), w_ref[...], preferred_element_type=F32)
    a = u[:, :D_CONV]
    gt = u[:, D_CONV:2 * D_CONV]
    ug_ref[...] = (a * _sigmoid(gt)).astype(BF16)
    o = 2 * D_CONV
    q = u[:, o:o + D_ATTN]
    k = u[:, o + D_ATTN:o + 2 * D_ATTN]
    v = u[:, o + 2 * D_ATTN:]
    qms = jnp.dot((q * q).astype(BF16), seg_ref[...], preferred_element_type=F32)
    kms = jnp.dot((k * k).astype(BF16), seg_ref[...], preferred_element_type=F32)
    q_ref[...] = (q * lax.rsqrt(qms + EPS) * qg_ref[...]).astype(BF16)
    k_ref[...] = (k * lax.rsqrt(kms + EPS) * kg_ref[...]).astype(BF16)
    v_ref[...] = v.astype(BF16)


def _inproj(x, mod3, norm_g, w_in_bf, seg, qg, kg):
    row = lambda i: (i, 0)
    const = lambda i: (0, 0)
    out_sd = jax.ShapeDtypeStruct((TOKENS, D_CONV), BF16)
    return pl.pallas_call(
        _inproj_kernel,
        grid=(TOKENS // TM,),
        in_specs=[
            pl.BlockSpec((TM, D_MODEL), row),
            pl.BlockSpec((1, 1, D_MODEL), lambda i: (i // TILES_PER_SEQ, 0, 0)),
            pl.BlockSpec((1, 1, D_MODEL), lambda i: (i // TILES_PER_SEQ, 0, 1)),
            pl.BlockSpec((1, D_MODEL), const),
            pl.BlockSpec((D_MODEL, D_IN), const),
            pl.BlockSpec((D_ATTN, D_ATTN), const),
            pl.BlockSpec((1, D_ATTN), const),
            pl.BlockSpec((1, D_ATTN), const),
        ],
        out_specs=[pl.BlockSpec((TM, D_CONV), row)] * 4,
        out_shape=[out_sd] * 4,
        compiler_params=pltpu.CompilerParams(
            dimension_semantics=("arbitrary",), vmem_limit_bytes=VMEM_LIMIT),
        name="inproj",
    )(x, mod3, mod3, norm_g, w_in_bf, seg, qg, kg)


def _conv_kernel(cur_ref, halo_ref, w_ref, b_ref, lg_ref, lb_ref, o_ref, xe_ref):
    i = pl.program_id(0)
    seq_start = (i % TILES_PER_SEQ) == 0
    halo = halo_ref[...].astype(F32)
    xe_ref[0, 0:HALO, :] = jnp.where(seq_start, 0.0, halo)
    xe_ref[0, HALO:, :] = cur_ref[...].astype(F32)
    ext = HALO + TM
    for s in range(1, SUBLANES):
        xe_ref[s, 0:ext - SUBLANES, :] = xe_ref[0, s:ext - SUBLANES + s, :]
    n_lane = D_CONV // LANES
    first_tap = HALO - (CONV_KERNEL - 1)

    def body(rc, carry):
        r0 = pl.multiple_of(rc * CONV_ROWS, CONV_ROWS)
        accs = []
        for c in range(n_lane):
            cs = slice(c * LANES, (c + 1) * LANES)
            acc = jnp.zeros((CONV_ROWS, LANES), F32) + b_ref[:, cs]
            for j in range(CONV_KERNEL):
                shift = (first_tap + j) % SUBLANES
                aligned = first_tap + j - shift
                acc = acc + w_ref[j:j + 1, cs] * xe_ref[shift, pl.ds(r0 + aligned, CONV_ROWS), cs]
            accs.append(acc)
        tot = accs[0].sum(axis=-1, keepdims=True)
        for c in range(1, n_lane):
            tot = tot + accs[c].sum(axis=-1, keepdims=True)
        mu = tot * (1.0 / D_CONV)
        cen = [a - mu for a in accs]
        var = (cen[0] * cen[0]).sum(axis=-1, keepdims=True)
        for c in range(1, n_lane):
            var = var + (cen[c] * cen[c]).sum(axis=-1, keepdims=True)
        inv = lax.rsqrt(var * (1.0 / D_CONV) + EPS)
        for c in range(n_lane):
            cs = slice(c * LANES, (c + 1) * LANES)
            y = cen[c] * inv * lg_ref[:, cs] + lb_ref[:, cs]
            o_ref[pl.ds(r0, CONV_ROWS), cs] = (y * _sigmoid(y)).astype(BF16)
        return carry

    lax.fori_loop(0, TM // CONV_ROWS, body, 0)


def _conv(ug, conv_w, conv_b, ln_g, ln_b):
    const = lambda i: (0, 0)
    halo_per_tile = TM // HALO
    return pl.pallas_call(
        _conv_kernel,
        grid=(TOKENS // TM,),
        in_specs=[
            pl.BlockSpec((TM, D_CONV), lambda i: (i, 0)),
            pl.BlockSpec((HALO, D_CONV), lambda i: (jnp.maximum(i * halo_per_tile - 1, 0), 0)),
            pl.BlockSpec((CONV_KERNEL, D_CONV), const),
            pl.BlockSpec((1, D_CONV), const),
            pl.BlockSpec((1, D_CONV), const),
            pl.BlockSpec((1, D_CONV), const),
        ],
        out_specs=pl.BlockSpec((TM, D_CONV), lambda i: (i, 0)),
        out_shape=jax.ShapeDtypeStruct((TOKENS, D_CONV), BF16),
        scratch_shapes=[pltpu.VMEM((SUBLANES, HALO + TM, D_CONV), F32)],
        compiler_params=pltpu.CompilerParams(
            dimension_semantics=("arbitrary",), vmem_limit_bytes=VMEM_LIMIT),
        name="conv_mixer",
    )(ug, ug, conv_w, conv_b, ln_g, ln_b)


def _attn_kernel(q_ref, kp_ref, kc_ref, vp_ref, vc_ref, bias_ref, o_ref, kw_ref, vw_ref):
    i = pl.program_id(0)
    first_valid = jnp.where((i % TILES_PER_SEQ) == 0, TM, 0)
    kw_ref[0:TM, :] = kp_ref[...]
    kw_ref[TM:, :] = kc_ref[...]
    vw_ref[0:TM, :] = vp_ref[...]
    vw_ref[TM:, :] = vc_ref[...]
    col = lax.broadcasted_iota(jnp.int32, (Q_PAIR, KEY_WIN), 1)

    def body(cp, carry):
        r0 = pl.multiple_of(cp * Q_PAIR, Q_PAIR)
        valid = (col + r0) >= first_valid
        for h in range(N_HEADS):
            hs = slice(h * HEAD_DIM, (h + 1) * HEAD_DIM)
            qh = q_ref[pl.ds(r0, Q_PAIR), hs]
            kb = kw_ref[pl.ds(r0, KEY_WIN), hs]
            vb = vw_ref[pl.ds(r0, KEY_WIN), hs]
            s = lax.dot_general(qh, kb, (((1,), (1,)), ((), ())), preferred_element_type=F32)
            s = jnp.where(valid, s + bias_ref[h], NEG_INF)
            m = jnp.max(s, axis=-1, keepdims=True)
            e = jnp.exp(s - m)
            l = jnp.sum(e, axis=-1, keepdims=True)
            o = jnp.dot(e.astype(BF16), vb, preferred_element_type=F32) * (1.0 / l)
            o_ref[pl.ds(r0, Q_PAIR), hs] = o.astype(BF16)
        return carry

    lax.fori_loop(0, PAIRS_PER_TILE, body, 0)


def _attention(q, k, v, bias2):
    cur = lambda i: (i, 0)
    prev = lambda i: (jnp.maximum(i - 1, 0), 0)
    blk = (TM, D_ATTN)
    return pl.pallas_call(
        _attn_kernel,
        grid=(TOKENS // TM,),
        in_specs=[
            pl.BlockSpec(blk, cur),
            pl.BlockSpec(blk, prev), pl.BlockSpec(blk, cur),
            pl.BlockSpec(blk, prev), pl.BlockSpec(blk, cur),
            pl.BlockSpec((N_HEADS, Q_PAIR, KEY_WIN), lambda i: (0, 0, 0)),
        ],
        out_specs=pl.BlockSpec(blk, cur),
        out_shape=jax.ShapeDtypeStruct((TOKENS, D_ATTN), BF16),
        scratch_shapes=[pltpu.VMEM((2 * TM, D_ATTN), BF16), pltpu.VMEM((2 * TM, D_ATTN), BF16)],
        compiler_params=pltpu.CompilerParams(
            dimension_semantics=("arbitrary",), vmem_limit_bytes=VMEM_LIMIT),
        name="chunk_attention",
    )(q, k, k, v, v, bias2)


def _store_row_tiles(ref, base, rows, value):
    for c in range(ROW_TILE):
        ref[pl.ds(base + c, rows, stride=ROW_TILE), :] = value[:, c * LANES:(c + 1) * LANES]


def _load_row_tile_chunk(ref, base, rows, c):
    return ref[pl.ds(base + c, rows, stride=ROW_TILE), :]


def _tile_copy(src_ref, src_row, dst_ref, dst_row, sem):
    return pltpu.make_async_copy(
        src_ref.at[pl.ds(src_row * ROW_TILE, ROW_TILE), :],
        dst_ref.at[pl.ds(dst_row * ROW_TILE, ROW_TILE), :], sem)


def _outproj_kernel(yc_ref, ya_ref, x_ref, g1_ref, wt_ref, wb_ref, sh_ref, sc_ref, g_ref,
                    rwh_ref, rwl_ref, rb_ref, tri_ref,
                    x1_ref, dest_ref, gate_ref, cnt_ref, xs_hbm,
                    hbuf, dvm, dsm, cvm, csm, zbuf, carry, sem_idx, sem_push, sem_zero):
    i = pl.program_id(0)
    n = pl.num_programs(0)
    slot = i % 2

    @pl.when(i == 0)
    def _():
        carry[...] = jnp.zeros_like(carry)

    mixed = (jnp.dot(yc_ref[...], wt_ref[...], preferred_element_type=F32)
             + jnp.dot(ya_ref[...], wb_ref[...], preferred_element_type=F32))
    x1 = x_ref[...] + g1_ref[0] * mixed
    x1_ref[...] = x1
    h2 = _modulated_norm(x1, g_ref[...], sc_ref[0], sh_ref[0])

    hi = h2.astype(BF16)
    lo = (h2 - hi.astype(F32)).astype(BF16)
    logits = (jnp.dot(hi, rwh_ref[...], preferred_element_type=F32)
              + jnp.dot(hi, rwl_ref[...], preferred_element_type=F32)
              + jnp.dot(lo, rwh_ref[...], preferred_element_type=F32)) + rb_ref[...]
    lt = logits.T[:N_EXPERTS, :]

    eid = lax.broadcasted_iota(I32, (N_EXPERTS, TM), 0)
    picks, onehots, exps = [], [], []
    top0 = None
    for kk in range(TOP_K):
        m = jnp.max(lt, axis=0, keepdims=True)
        am = jnp.min(jnp.where(lt == m, eid, N_EXPERTS), axis=0, keepdims=True)
        oh = eid == am
        lt = jnp.where(oh, NEG_INF, lt)
        if kk == 0:
            top0 = m
        picks.append(am)
        onehots.append(oh)
        exps.append(jnp.exp(m - top0))
    denom = exps[0] + exps[1] + exps[2] + exps[3]
    gate_ref[...] = jnp.concatenate(exps, axis=0) * (1.0 / denom)

    oh_all = (onehots[0].astype(F32) + onehots[1].astype(F32)
              + onehots[2].astype(F32) + onehots[3].astype(F32))
    before = jnp.dot(oh_all.astype(BF16), tri_ref[...], preferred_element_type=F32) + carry[:, 0:1]
    dests = []
    for kk in range(TOP_K):
        rank = jnp.sum(jnp.where(onehots[kk], before, 0.0), axis=0, keepdims=True)
        dests.append(picks[kk] * EXPERT_CAP + rank.astype(I32))
    dest = jnp.concatenate(dests, axis=0)
    dest_ref[...] = dest
    carry[...] = carry[...] + jnp.sum(oh_all, axis=1, keepdims=True)
    cnt_ref[...] = carry[...]

    dvm[...] = jnp.concatenate([dest, jnp.zeros((SUBLANES - TOP_K, TM), I32)], axis=0)
    idx_copy = pltpu.make_async_copy(dvm, dsm, sem_idx)
    idx_copy.start()

    hb = pl.multiple_of(slot * (TM * ROW_TILE), TM * ROW_TILE)
    _store_row_tiles(hbuf, hb, TM, h2)

    def wait_push(s):
        base = pl.multiple_of(s * (TM * ROW_TILE), TM * ROW_TILE)
        for _ in range(TOP_K):
            pltpu.make_async_copy(hbuf.at[pl.ds(base, TM * ROW_TILE), :],
                                  xs_hbm.at[pl.ds(0, TM * ROW_TILE), :], sem_push.at[s]).wait()

    @pl.when(i > 0)
    def _():
        wait_push(1 - slot)

    idx_copy.wait()
    per_iter = DMA_UNROLL // TOP_K

    def push_body(r, c):
        for u in range(DMA_UNROLL):
            t = r * per_iter + u // TOP_K
            d = dsm[u % TOP_K, t]
            _tile_copy(hbuf, slot * TM + t, xs_hbm, d, sem_push.at[slot]).start()
        return c

    lax.fori_loop(0, TM // per_iter, push_body, 0)

    @pl.when(i == n - 1)
    def _():
        wait_push(slot)
        zbuf[...] = jnp.zeros_like(zbuf)
        cvm[...] = carry[...].astype(I32)
        cnt_copy = pltpu.make_async_copy(cvm, csm, sem_idx)
        cnt_copy.start()
        cnt_copy.wait()

        def per_expert(e, c, wait):
            cnt = csm[e, 0]
            end = (cnt + MOE_M - 1) // MOE_M * MOE_M

            def row(r, c2):
                cp = _tile_copy(zbuf, 0, xs_hbm, e * EXPERT_CAP + r, sem_zero)
                if wait:
                    cp.wait()
                else:
                    cp.start()
                return c2

            lax.fori_loop(cnt, end, row, 0)
            return c

        lax.fori_loop(0, N_EXPERTS, lambda e, c: per_expert(e, c, False), 0)
        lax.fori_loop(0, N_EXPERTS, lambda e, c: per_expert(e, c, True), 0)


def _outproj(yc, ya, x, mod3, w_top, w_bot, norm_g, rw_hi, rw_lo, rb, tri):
    row = lambda i: (i, 0)
    col = lambda i: (0, i)
    const = lambda i: (0, 0)
    modspec = lambda j: pl.BlockSpec((1, 1, D_MODEL), lambda i: (i // TILES_PER_SEQ, 0, j))
    return pl.pallas_call(
        _outproj_kernel,
        grid=(TOKENS // TM,),
        in_specs=[
            pl.BlockSpec((TM, D_CONV), row),
            pl.BlockSpec((TM, D_ATTN), row),
            pl.BlockSpec((TM, D_MODEL), row),
            modspec(2),
            pl.BlockSpec((D_CONV, D_MODEL), const),
            pl.BlockSpec((D_ATTN, D_MODEL), const),
            modspec(3),
            modspec(4),
            pl.BlockSpec((1, D_MODEL), const),
            pl.BlockSpec((D_MODEL, LANES), const),
            pl.BlockSpec((D_MODEL, LANES), const),
            pl.BlockSpec((1, LANES), const),
            pl.BlockSpec((TM, TM), const),
        ],
        out_specs=[
            pl.BlockSpec((TM, D_MODEL), row),
            pl.BlockSpec((TOP_K, TM), col),
            pl.BlockSpec((TOP_K, TM), col),
            pl.BlockSpec((N_EXPERTS, LANES), const),
            pl.BlockSpec(memory_space=pl.ANY),
        ],
        out_shape=[
            jax.ShapeDtypeStruct((TOKENS, D_MODEL), F32),
            jax.ShapeDtypeStruct((TOP_K, TOKENS), I32),
            jax.ShapeDtypeStruct((TOP_K, TOKENS), F32),
            jax.ShapeDtypeStruct((N_EXPERTS, LANES), F32),
            jax.ShapeDtypeStruct((N_EXPERTS * EXPERT_CAP * ROW_TILE, LANES), F32),
        ],
        scratch_shapes=[
            pltpu.VMEM((2 * TM * ROW_TILE, LANES), F32),
            pltpu.VMEM((SUBLANES, TM), I32),
            pltpu.SMEM((SUBLANES, TM), I32),
            pltpu.VMEM((N_EXPERTS, LANES), I32),
            pltpu.SMEM((N_EXPERTS, LANES), I32),
            pltpu.VMEM((ROW_TILE, LANES), F32),
            pltpu.VMEM((N_EXPERTS, LANES), F32),
            pltpu.SemaphoreType.DMA(()),
            pltpu.SemaphoreType.DMA((2,)),
            pltpu.SemaphoreType.DMA(()),
        ],
        compiler_params=pltpu.CompilerParams(
            dimension_semantics=("arbitrary",), vmem_limit_bytes=VMEM_LIMIT),
        name="outproj_router",
    )(yc, ya, x, mod3, w_top, w_bot, mod3, mod3, norm_g, rw_hi, rw_lo, rb, tri)


def _expert_kernel(be_ref, br_ref, nu_ref, x_ref, w1_ref, b1_ref, w2_ref, b2_ref, o_ref, w1b, w2b):
    i = pl.program_id(0)

    @pl.when(i < nu_ref[0])
    def _():
        prev_e = be_ref[jnp.maximum(i - 1, 0)]

        @pl.when(jnp.logical_or(i == 0, be_ref[i] != prev_e))
        def _():
            w1b[...] = w1_ref[0, 0].astype(BF16)
            w2b[...] = w2_ref[0, 0].astype(BF16)

        xs = jnp.concatenate(
            [_load_row_tile_chunk(x_ref, 0, MOE_M, c) for c in range(ROW_TILE)], axis=1).astype(BF16)
        u = jnp.dot(xs, w1b[...], preferred_element_type=F32) + b1_ref[0, 0]
        glu = jnp.minimum(u[:, :D_EXPERT], SWIGLU_LIMIT)
        lin = jnp.clip(u[:, D_EXPERT:], -SWIGLU_LIMIT, SWIGLU_LIMIT)
        act = glu * _sigmoid(SWIGLU_ALPHA * glu) * (lin + 1.0)
        y = jnp.dot(act.astype(BF16), w2b[...], preferred_element_type=F32) + b2_ref[0, 0]
        _store_row_tiles(o_ref, 0, MOE_M, y)


def _experts(layer, block_exp, block_row, n_used, xs, w1, b1, w2, b2):
    wmap = lambda i, be, br, nu: (layer, be[i], 0, 0)
    rmap = lambda i, be, br, nu: (br[i], 0)
    grid_spec = pltpu.PrefetchScalarGridSpec(
        num_scalar_prefetch=3,
        grid=(N_BLOCKS,),
        in_specs=[
            pl.BlockSpec((MOE_M * ROW_TILE, LANES), rmap),
            pl.BlockSpec((1, 1, D_MODEL, 2 * D_EXPERT), wmap),
            pl.BlockSpec((1, 1, 1, 2 * D_EXPERT), wmap),
            pl.BlockSpec((1, 1, D_EXPERT, D_MODEL), wmap),
            pl.BlockSpec((1, 1, 1, D_MODEL), wmap),
        ],
        out_specs=pl.BlockSpec((MOE_M * ROW_TILE, LANES), rmap),
        scratch_shapes=[
            pltpu.VMEM((D_MODEL, 2 * D_EXPERT), BF16),
            pltpu.VMEM((D_EXPERT, D_MODEL), BF16),
        ],
    )
    return pl.pallas_call(
        _expert_kernel,
        grid_spec=grid_spec,
        out_shape=jax.ShapeDtypeStruct((N_EXPERTS * EXPERT_CAP * ROW_TILE, LANES), F32),
        compiler_params=pltpu.CompilerParams(
            dimension_semantics=("arbitrary",), vmem_limit_bytes=VMEM_LIMIT),
        name="experts",
    )(block_exp, block_row, n_used, xs,
      w1, b1.reshape(DEPTH, N_EXPERTS, 1, 2 * D_EXPERT), w2,
      b2.reshape(DEPTH, N_EXPERTS, 1, D_MODEL))


def _block_tables(counts):
    nblk = (counts + MOE_M - 1) // MOE_M
    blk_end = jnp.cumsum(nblk)
    blk_start = blk_end - nblk
    n_used = blk_end[-1]
    step = jnp.minimum(jnp.arange(N_BLOCKS, dtype=I32), n_used - 1)
    block_exp = jnp.minimum(jnp.searchsorted(blk_end, step, side="right"), N_EXPERTS - 1).astype(I32)
    block_row = block_exp * CAP_BLOCKS + (step - blk_start[block_exp])
    return block_exp, block_row.astype(I32), n_used.astype(I32).reshape(1)


def _combine_kernel(idx0_ref, idxn_ref, ys_hbm, x1_ref, g2_ref, gate_ref, o_ref, buf, sem):
    i = pl.program_id(0)
    n = pl.num_programs(0)
    rows = TMC * TOP_K
    slot = i % 2

    def start_gather(idx_ref, s):
        def body(r, c):
            for u in range(DMA_UNROLL):
                j = r * DMA_UNROLL + u
                _tile_copy(ys_hbm, idx_ref[0, 0, j], buf, s * rows + j, sem.at[s]).start()
            return c

        lax.fori_loop(0, rows // DMA_UNROLL, body, 0)

    @pl.when(i == 0)
    def _():
        start_gather(idx0_ref, 0)

    @pl.when(i + 1 < n)
    def _():
        start_gather(idxn_ref, 1 - slot)

    base = pl.multiple_of(slot * (rows * ROW_TILE), rows * ROW_TILE)
    pltpu.make_async_copy(ys_hbm.at[pl.ds(0, rows * ROW_TILE), :],
                          buf.at[pl.ds(base, rows * ROW_TILE), :], sem.at[slot]).wait()
    gates = gate_ref[...]
    g2 = g2_ref[0]
    for c in range(ROW_TILE):
        cs = slice(c * LANES, (c + 1) * LANES)
        acc = jnp.zeros((TMC, LANES), F32)
        for kk in range(TOP_K):
            acc = acc + gates[:, kk:kk + 1] * _load_row_tile_chunk(
                buf, base + kk * TMC * ROW_TILE, TMC, c)
        o_ref[:, cs] = x1_ref[:, cs] + g2[:, cs] * acc


def _combine(dest3, ys, x1, mod3, gates):
    n_tiles = TOKENS // TMC
    tiles_per_seq = SEQ // TMC
    rows = TMC * TOP_K
    return pl.pallas_call(
        _combine_kernel,
        grid=(n_tiles,),
        in_specs=[
            pl.BlockSpec((1, 1, rows), lambda i: (0, 0, 0), memory_space=pltpu.SMEM),
            pl.BlockSpec((1, 1, rows), lambda i: (jnp.minimum(i + 1, n_tiles - 1), 0, 0),
                         memory_space=pltpu.SMEM),
            pl.BlockSpec(memory_space=pl.ANY),
            pl.BlockSpec((TMC, D_MODEL), lambda i: (i, 0)),
            pl.BlockSpec((1, 1, D_MODEL), lambda i: (i // tiles_per_seq, 0, 5)),
            pl.BlockSpec((TMC, TOP_K), lambda i: (i, 0)),
        ],
        out_specs=pl.BlockSpec((TMC, D_MODEL), lambda i: (i, 0)),
        out_shape=jax.ShapeDtypeStruct((TOKENS, D_MODEL), F32),
        scratch_shapes=[
            pltpu.VMEM((2 * rows * ROW_TILE, LANES), F32),
            pltpu.SemaphoreType.DMA((2,)),
        ],
        compiler_params=pltpu.CompilerParams(
            dimension_semantics=("arbitrary",), vmem_limit_bytes=VMEM_LIMIT),
        name="moe_combine",
    )(dest3, dest3, ys, x1, mod3, gates)


_I = np.arange(Q_PAIR)[:, None]
_K = np.arange(KEY_WIN)[None, :]
_BAND_OK = (_K // CHUNK >= _I // CHUNK) & (_K // CHUNK <= _I // CHUNK + LEFT_CHUNKS)
_SEG = np.kron(np.eye(N_HEADS), np.full((HEAD_DIM, HEAD_DIM), 1.0 / HEAD_DIM)).astype(np.float32)
_TRI = (np.arange(TM)[:, None] < np.arange(TM)[None, :]).astype(np.float32)


def _bias_table(rel_bias_l):
    n_far = KEY_WIN - 1 + Q_PAIR - 2 * REL_CLIP
    ext = jnp.concatenate(
        [rel_bias_l[:, 1:], jnp.repeat(rel_bias_l[:, 2 * REL_CLIP:], n_far, axis=1)], axis=1)
    rev = ext[:, ::-1]
    rows = [rev[:, Q_PAIR - 1 - i:Q_PAIR - 1 - i + KEY_WIN] for i in range(Q_PAIR)]
    return jnp.where(_BAND_OK[None], jnp.stack(rows, axis=1), NEG_INF).astype(F32)


def kernel(x, c, ada_w, ada_b, norm1_g, w_in, conv_w, conv_b, conv_ln_g, conv_ln_b, q_norm_g,
           k_norm_g, rel_bias, w_out, norm2_g, router_w, router_b, exp_w1, exp_b1, exp_w2, exp_b2):
    xf = x.reshape(TOKENS, D_MODEL)
    mod = _ada_mod(c, ada_w, ada_b)
    seg = jnp.asarray(_SEG, BF16)
    tri = jnp.asarray(_TRI, BF16)
    pad_e = ((0, 0), (0, LANES - N_EXPERTS))
    for l in range(DEPTH):
        mod3 = mod[l].reshape(BATCH, 1, 6 * D_MODEL)
        qg = (jnp.tile(q_norm_g[l], N_HEADS) * (HEAD_DIM ** -0.5)).reshape(1, D_ATTN)
        kg = jnp.tile(k_norm_g[l], N_HEADS).reshape(1, D_ATTN)
        ug, q, k, v = _inproj(xf, mod3, norm1_g[l].reshape(1, D_MODEL), w_in[l].astype(BF16),
                              seg, qg, kg)
        y_conv = _conv(ug, conv_w[l], conv_b[l].reshape(1, D_CONV),
                       conv_ln_g[l].reshape(1, D_CONV), conv_ln_b[l].reshape(1, D_CONV))
        y_attn = _attention(q, k, v, _bias_table(rel_bias[l]))
        w_out_bf = w_out[l].astype(BF16)
        rw_hi = router_w[l].astype(BF16)
        rw_lo = (router_w[l] - rw_hi.astype(F32)).astype(BF16)
        x1, dest, gates, cnt, xs = _outproj(
            y_conv, y_attn, xf, mod3, w_out_bf[:D_CONV], w_out_bf[D_CONV:],
            norm2_g[l].reshape(1, D_MODEL), jnp.pad(rw_hi, pad_e), jnp.pad(rw_lo, pad_e),
            jnp.pad(router_b[l].reshape(1, N_EXPERTS), pad_e), tri)
        block_exp, block_row, n_used = _block_tables(cnt[:, 0].astype(I32))
        ys = _experts(l, block_exp, block_row, n_used, xs, exp_w1, exp_b1, exp_w2, exp_b2)
        dest3 = dest.reshape(TOP_K, TOKENS // TMC, TMC).transpose(1, 0, 2).reshape(
            TOKENS // TMC, 1, TMC * TOP_K)
        xf = _combine(dest3, ys, x1, mod3, gates.T)
    return xf.reshape(BATCH, SEQ, D_MODEL)
```

```python
import jax
import jax.numpy as jnp
import numpy as np
from jax import lax
from jax.experimental import pallas as pl
from jax.experimental.pallas import tpu as pltpu

D_MODEL = 1024
BATCH = 8
SEQ = 2048
DEPTH = 4
TOKENS = BATCH * SEQ

CHUNK = 64
D_CONV = D_MODEL // 2
D_ATTN = D_MODEL - D_CONV
N_HEADS = 8
HEAD_DIM = D_ATTN // N_HEADS
CONV_KERNEL = 31
LEFT_CHUNKS = 8
REL_CLIP = 128
N_EXPERTS = 32
TOP_K = 4
D_EXPERT = D_MODEL
SWIGLU_ALPHA = 1.702
SWIGLU_LIMIT = 7.0
EPS = 1e-6
D_IN = 2 * D_CONV + 3 * D_ATTN

F32 = jnp.float32
BF16 = jnp.bfloat16
I32 = jnp.int32
NEG_INF = float("-inf")

LANES = 128
SUBLANES = 8
TM = 512
TILES_PER_SEQ = SEQ // TM
Q_PAIR = 2 * CHUNK
KEY_WIN = (LEFT_CHUNKS + 2) * CHUNK
PAIRS_PER_TILE = TM // Q_PAIR
HALO = 32
CONV_ROWS = 32
MOE_M = 256
EXPERT_CAP = TOKENS
CAP_BLOCKS = EXPERT_CAP // MOE_M
N_BLOCKS = TOKENS * TOP_K // MOE_M + N_EXPERTS
ROW_TILE = D_MODEL // LANES
TMC = 256
DMA_UNROLL = 8

VMEM_LIMIT = 56 * 1024 * 1024


def _sigmoid(x):
    return 1.0 / (1.0 + jnp.exp(-x))


def _ada_kernel(c_ref, w_ref, b_ref, o_ref):
    c = c_ref[...]
    ca = (c * _sigmoid(c)).astype(BF16)
    o_ref[0] = jnp.dot(ca, w_ref[0].astype(BF16), preferred_element_type=F32) + b_ref[0]


def _ada_mod(c, ada_w, ada_b):
    n_tile = 1536
    return pl.pallas_call(
        _ada_kernel,
        grid=(DEPTH, 6 * D_MODEL // n_tile),
        in_specs=[
            pl.BlockSpec((BATCH, D_MODEL), lambda l, n: (0, 0)),
            pl.BlockSpec((1, D_MODEL, n_tile), lambda l, n: (l, 0, n)),
            pl.BlockSpec((1, 1, n_tile), lambda l, n: (l, 0, n)),
        ],
        out_specs=pl.BlockSpec((1, BATCH, n_tile), lambda l, n: (l, 0, n)),
        out_shape=jax.ShapeDtypeStruct((DEPTH, BATCH, 6 * D_MODEL), F32),
        compiler_params=pltpu.CompilerParams(
            dimension_semantics=("arbitrary", "arbitrary"), vmem_limit_bytes=VMEM_LIMIT),
        name="ada_mod",
    )(c, ada_w, ada_b.reshape(DEPTH, 1, 6 * D_MODEL))


def _modulated_norm(x, g, sc, sh):
    ms = jnp.mean(x * x, axis=-1, keepdims=True)
    return x * lax.rsqrt(ms + EPS) * g * (1.0 + sc) + sh


def _inproj_kernel(x_ref, sh_ref, sc_ref, g_ref, w_ref, seg_ref, qg_ref, kg_ref,
                   ug_ref, q_ref, k_ref, v_ref):
    h = _modulated_norm(x_ref[...], g_ref[...], sc_ref[0], sh_ref[0])
    u = jnp.dot(h.astype(BF16), w_ref[...], preferred_element_type=F32)
    a = u[:, :D_CONV]
    gt = u[:, D_CONV:2 * D_CONV]
    ug_ref[...] = (a * _sigmoid(gt)).astype(BF16)
    o = 2 * D_CONV
    q = u[:, o:o + D_ATTN]
    k = u[:, o + D_ATTN:o + 2 * D_ATTN]
    v = u[:, o + 2 * D_ATTN:]
    qms = jnp.dot((q * q).astype(BF16), seg_ref[...], preferred_element_type=F32)
    kms = jnp.dot((k * k).astype(BF16), seg_ref[...], preferred_element_type=F32)
    q_ref[...] = (q * lax.rsqrt(qms + EPS) * qg_ref[...]).astype(BF16)
    k_ref[...] = (k * lax.rsqrt(kms + EPS) * kg_ref[...]).astype(BF16)
    v_ref[...] = v.astype(BF16)


def _inproj(x, mod3, norm_g, w_in_bf, seg, qg, kg):
    row = lambda i: (i, 0)
    const = lambda i: (0, 0)
    out_sd = jax.ShapeDtypeStruct((TOKENS, D_CONV), BF16)
    return pl.pallas_call(
        _inproj_kernel,
        grid=(TOKENS // TM,),
        in_specs=[
            pl.BlockSpec((TM, D_MODEL), row),
            pl.BlockSpec((1, 1, D_MODEL), lambda i: (i // TILES_PER_SEQ, 0, 0)),
            pl.BlockSpec((1, 1, D_MODEL), lambda i: (i // TILES_PER_SEQ, 0, 1)),
            pl.BlockSpec((1, D_MODEL), const),
            pl.BlockSpec((D_MODEL, D_IN), const),
            pl.BlockSpec((D_ATTN, D_ATTN), const),
            pl.BlockSpec((1, D_ATTN), const),
            pl.BlockSpec((1, D_ATTN), const),
        ],
        out_specs=[pl.BlockSpec((TM, D_CONV), row)] * 4,
        out_shape=[out_sd] * 4,
        compiler_params=pltpu.CompilerParams(
            dimension_semantics=("arbitrary",), vmem_limit_bytes=VMEM_LIMIT),
        name="inproj",
    )(x, mod3, mod3, norm_g, w_in_bf, seg, qg, kg)


def _conv_kernel(cur_ref, halo_ref, w_ref, b_ref, lg_ref, lb_ref, o_ref, xe_ref):
    i = pl.program_id(0)
    seq_start = (i % TILES_PER_SEQ) == 0
    halo = halo_ref[...].astype(F32)
    xe_ref[0, 0:HALO, :] = jnp.where(seq_start, 0.0, halo)
    xe_ref[0, HALO:, :] = cur_ref[...].astype(F32)
    ext = HALO + TM
    for s in range(1, SUBLANES):
        xe_ref[s, 0:ext - SUBLANES, :] = xe_ref[0, s:ext - SUBLANES + s, :]
    n_lane = D_CONV // LANES
    first_tap = HALO - (CONV_KERNEL - 1)

    def body(rc, carry):
        r0 = pl.multiple_of(rc * CONV_ROWS, CONV_ROWS)
        lane_slices = [slice(c * LANES, (c + 1) * LANES) for c in range(n_lane)]
        accs = [jnp.zeros((CONV_ROWS, LANES), F32) + b_ref[:, cs] for cs in lane_slices]
        for j in range(CONV_KERNEL):
            shift = (first_tap + j) % SUBLANES
            aligned = first_tap + j - shift
            for c, cs in enumerate(lane_slices):
                accs[c] = accs[c] + w_ref[j:j + 1, cs] * xe_ref[shift, pl.ds(r0 + aligned, CONV_ROWS), cs]
        tot = accs[0].sum(axis=-1, keepdims=True)
        for c in range(1, n_lane):
            tot = tot + accs[c].sum(axis=-1, keepdims=True)
        mu = tot * (1.0 / D_CONV)
        cen = [a - mu for a in accs]
        var = (cen[0] * cen[0]).sum(axis=-1, keepdims=True)
        for c in range(1, n_lane):
            var = var + (cen[c] * cen[c]).sum(axis=-1, keepdims=True)
        inv = lax.rsqrt(var * (1.0 / D_CONV) + EPS)
        for c in range(n_lane):
            cs = slice(c * LANES, (c + 1) * LANES)
            y = cen[c] * inv * lg_ref[:, cs] + lb_ref[:, cs]
            o_ref[pl.ds(r0, CONV_ROWS), cs] = (y * _sigmoid(y)).astype(BF16)
        return carry

    lax.fori_loop(0, TM // CONV_ROWS, body, 0)


def _conv(ug, conv_w, conv_b, ln_g, ln_b):
    const = lambda i: (0, 0)
    halo_per_tile = TM // HALO
    return pl.pallas_call(
        _conv_kernel,
        grid=(TOKENS // TM,),
        in_specs=[
            pl.BlockSpec((TM, D_CONV), lambda i: (i, 0)),
            pl.BlockSpec((HALO, D_CONV), lambda i: (jnp.maximum(i * halo_per_tile - 1, 0), 0)),
            pl.BlockSpec((CONV_KERNEL, D_CONV), const),
            pl.BlockSpec((1, D_CONV), const),
            pl.BlockSpec((1, D_CONV), const),
            pl.BlockSpec((1, D_CONV), const),
        ],
        out_specs=pl.BlockSpec((TM, D_CONV), lambda i: (i, 0)),
        out_shape=jax.ShapeDtypeStruct((TOKENS, D_CONV), BF16),
        scratch_shapes=[pltpu.VMEM((SUBLANES, HALO + TM, D_CONV), F32)],
        compiler_params=pltpu.CompilerParams(
            dimension_semantics=("arbitrary",), vmem_limit_bytes=VMEM_LIMIT),
        name="conv_mixer",
    )(ug, ug, conv_w, conv_b, ln_g, ln_b)


def _attn_kernel(q_ref, kp_ref, kc_ref, vp_ref, vc_ref, bias_ref, o_ref, kw_ref, vt_ref):
    i = pl.program_id(0)
    seq_start = (i % TILES_PER_SEQ) == 0
    kw_ref[0:TM, :] = kp_ref[...]
    kw_ref[TM:, :] = kc_ref[...]
    vt_ref[:, 0:TM] = vp_ref[...].T
    vt_ref[:, TM:] = vc_ref[...].T
    pair = 2 * HEAD_DIM
    low = lax.broadcasted_iota(jnp.int32, (Q_PAIR, pair), 1) < HEAD_DIM

    def pair_block(cp, hp, k_lo):
        r0 = cp * Q_PAIR
        ps = slice(hp * pair, (hp + 1) * pair)
        q2 = q_ref[r0:r0 + Q_PAIR, ps]
        zero = jnp.zeros_like(q2)
        qm = jnp.concatenate([jnp.where(low, q2, zero), jnp.where(low, zero, q2)], axis=0)
        k2 = kw_ref[r0 + k_lo:r0 + KEY_WIN, ps]
        st = lax.dot_general(k2, qm, (((1,), (1,)), ((), ())), preferred_element_type=F32)
        st = st + bias_ref[hp, k_lo:, :]
        m = jnp.max(st, axis=0, keepdims=True)
        e = jnp.exp(st - m)
        l = jnp.sum(e, axis=0, keepdims=True)
        ot = jnp.dot(vt_ref[ps, r0 + k_lo:r0 + KEY_WIN], e.astype(BF16),
                     preferred_element_type=F32) * (1.0 / l)
        o2t = jnp.concatenate([ot[:HEAD_DIM, :Q_PAIR], ot[HEAD_DIM:, Q_PAIR:]], axis=0)
        o_ref[r0:r0 + Q_PAIR, ps] = o2t.T.astype(BF16)

    @pl.when(seq_start)
    def _():
        for cp in range(PAIRS_PER_TILE):
            for hp in range(N_HEADS // 2):
                pair_block(cp, hp, TM - cp * Q_PAIR)

    @pl.when(jnp.logical_not(seq_start))
    def _():
        for cp in range(PAIRS_PER_TILE):
            for hp in range(N_HEADS // 2):
                pair_block(cp, hp, 0)


def _attention(q, k, v, bias2):
    cur = lambda i: (i, 0)
    prev = lambda i: (jnp.maximum(i - 1, 0), 0)
    blk = (TM, D_ATTN)
    return pl.pallas_call(
        _attn_kernel,
        grid=(TOKENS // TM,),
        in_specs=[
            pl.BlockSpec(blk, cur),
            pl.BlockSpec(blk, prev), pl.BlockSpec(blk, cur),
            pl.BlockSpec(blk, prev), pl.BlockSpec(blk, cur),
            pl.BlockSpec((N_HEADS // 2, KEY_WIN, 2 * Q_PAIR), lambda i: (0, 0, 0)),
        ],
        out_specs=pl.BlockSpec(blk, cur),
        out_shape=jax.ShapeDtypeStruct((TOKENS, D_ATTN), BF16),
        scratch_shapes=[pltpu.VMEM((2 * TM, D_ATTN), BF16), pltpu.VMEM((D_ATTN, 2 * TM), BF16)],
        compiler_params=pltpu.CompilerParams(
            dimension_semantics=("arbitrary",), vmem_limit_bytes=VMEM_LIMIT),
        name="chunk_attention",
    )(q, k, k, v, v, bias2)


def _store_row_tiles(ref, base, rows, value):
    for c in range(ROW_TILE):
        ref[pl.ds(base + c, rows, stride=ROW_TILE), :] = value[:, c * LANES:(c + 1) * LANES]


def _load_row_tile_chunk(ref, base, rows, c):
    return ref[pl.ds(base + c, rows, stride=ROW_TILE), :]


def _tile_copy(src_ref, src_row, dst_ref, dst_row, sem):
    return pltpu.make_async_copy(
        src_ref.at[pl.ds(src_row * ROW_TILE, ROW_TILE), :],
        dst_ref.at[pl.ds(dst_row * ROW_TILE, ROW_TILE), :], sem)


def _outproj_kernel(yc_ref, ya_ref, x_ref, g1_ref, wt_ref, wb_ref, sh_ref, sc_ref, g_ref,
                    rwh_ref, rwl_ref, rb_ref, tri_ref,
                    x1_ref, dest_ref, gate_ref, cnt_ref, xs_hbm,
                    hbuf, dvm, dsm, cvm, csm, zbuf, carry, sem_idx, sem_push, sem_zero):
    i = pl.program_id(0)
    n = pl.num_programs(0)
    slot = i % 2

    @pl.when(i == 0)
    def _():
        carry[...] = jnp.zeros_like(carry)

    mixed = (jnp.dot(yc_ref[...], wt_ref[...], preferred_element_type=F32)
             + jnp.dot(ya_ref[...], wb_ref[...], preferred_element_type=F32))
    x1 = x_ref[...] + g1_ref[0] * mixed
    x1_ref[...] = x1
    h2 = _modulated_norm(x1, g_ref[...], sc_ref[0], sh_ref[0])

    hi = h2.astype(BF16)
    lo = (h2 - hi.astype(F32)).astype(BF16)
    logits = (jnp.dot(hi, rwh_ref[...], preferred_element_type=F32)
              + jnp.dot(hi, rwl_ref[...], preferred_element_type=F32)
              + jnp.dot(lo, rwh_ref[...], preferred_element_type=F32)) + rb_ref[...]
    lt = logits.T[:N_EXPERTS, :]

    eid = lax.broadcasted_iota(I32, (N_EXPERTS, TM), 0)
    picks, onehots, exps = [], [], []
    top0 = None
    for kk in range(TOP_K):
        m = jnp.max(lt, axis=0, keepdims=True)
        am = jnp.min(jnp.where(lt == m, eid, N_EXPERTS), axis=0, keepdims=True)
        oh = eid == am
        lt = jnp.where(oh, NEG_INF, lt)
        if kk == 0:
            top0 = m
        picks.append(am)
        onehots.append(oh)
        exps.append(jnp.exp(m - top0))
    denom = exps[0] + exps[1] + exps[2] + exps[3]
    gate_ref[...] = jnp.concatenate(exps, axis=0) * (1.0 / denom)

    oh_all = (onehots[0].astype(F32) + onehots[1].astype(F32)
              + onehots[2].astype(F32) + onehots[3].astype(F32))
    before = jnp.dot(oh_all.astype(BF16), tri_ref[...], preferred_element_type=F32) + carry[:, 0:1]
    dests = []
    for kk in range(TOP_K):
        rank = jnp.sum(jnp.where(onehots[kk], before, 0.0), axis=0, keepdims=True)
        dests.append(picks[kk] * EXPERT_CAP + rank.astype(I32))
    dest = jnp.concatenate(dests, axis=0)
    dest_ref[...] = dest
    carry[...] = carry[...] + jnp.sum(oh_all, axis=1, keepdims=True)
    cnt_ref[...] = carry[...]

    dvm[...] = jnp.concatenate([dest, jnp.zeros((SUBLANES - TOP_K, TM), I32)], axis=0)
    idx_copy = pltpu.make_async_copy(dvm, dsm, sem_idx)
    idx_copy.start()

    hb = pl.multiple_of(slot * (TM * ROW_TILE), TM * ROW_TILE)
    _store_row_tiles(hbuf, hb, TM, h2)

    def wait_push(s):
        base = pl.multiple_of(s * (TM * ROW_TILE), TM * ROW_TILE)
        for _ in range(TOP_K):
            pltpu.make_async_copy(hbuf.at[pl.ds(base, TM * ROW_TILE), :],
                                  xs_hbm.at[pl.ds(0, TM * ROW_TILE), :], sem_push.at[s]).wait()

    @pl.when(i > 0)
    def _():
        wait_push(1 - slot)

    idx_copy.wait()
    per_iter = DMA_UNROLL // TOP_K

    def push_body(r, c):
        for u in range(DMA_UNROLL):
            t = r * per_iter + u // TOP_K
            d = dsm[u % TOP_K, t]
            _tile_copy(hbuf, slot * TM + t, xs_hbm, d, sem_push.at[slot]).start(priority=u % 2)
        return c

    lax.fori_loop(0, TM // per_iter, push_body, 0)

    @pl.when(i == n - 1)
    def _():
        wait_push(slot)
        zbuf[...] = jnp.zeros_like(zbuf)
        cvm[...] = carry[...].astype(I32)
        cnt_copy = pltpu.make_async_copy(cvm, csm, sem_idx)
        cnt_copy.start()
        cnt_copy.wait()

        def per_expert(e, c, wait):
            cnt = csm[e, 0]
            end = (cnt + MOE_M - 1) // MOE_M * MOE_M

            def row(r, c2):
                cp = _tile_copy(zbuf, 0, xs_hbm, e * EXPERT_CAP + r, sem_zero)
                if wait:
                    cp.wait()
                else:
                    cp.start()
                return c2

            lax.fori_loop(cnt, end, row, 0)
            return c

        lax.fori_loop(0, N_EXPERTS, lambda e, c: per_expert(e, c, False), 0)
        lax.fori_loop(0, N_EXPERTS, lambda e, c: per_expert(e, c, True), 0)


def _outproj(yc, ya, x, mod3, w_top, w_bot, norm_g, rw_hi, rw_lo, rb, tri):
    row = lambda i: (i, 0)
    col = lambda i: (0, i)
    const = lambda i: (0, 0)
    modspec = lambda j: pl.BlockSpec((1, 1, D_MODEL), lambda i: (i // TILES_PER_SEQ, 0, j))
    return pl.pallas_call(
        _outproj_kernel,
        grid=(TOKENS // TM,),
        in_specs=[
            pl.BlockSpec((TM, D_CONV), row),
            pl.BlockSpec((TM, D_ATTN), row),
            pl.BlockSpec((TM, D_MODEL), row),
            modspec(2),
            pl.BlockSpec((D_CONV, D_MODEL), const),
            pl.BlockSpec((D_ATTN, D_MODEL), const),
            modspec(3),
            modspec(4),
            pl.BlockSpec((1, D_MODEL), const),
            pl.BlockSpec((D_MODEL, LANES), const),
            pl.BlockSpec((D_MODEL, LANES), const),
            pl.BlockSpec((1, LANES), const),
            pl.BlockSpec((TM, TM), const),
        ],
        out_specs=[
            pl.BlockSpec((TM, D_MODEL), row),
            pl.BlockSpec((TOP_K, TM), col),
            pl.BlockSpec((TOP_K, TM), col),
            pl.BlockSpec((N_EXPERTS, LANES), const),
            pl.BlockSpec(memory_space=pl.ANY),
        ],
        out_shape=[
            jax.ShapeDtypeStruct((TOKENS, D_MODEL), F32),
            jax.ShapeDtypeStruct((TOP_K, TOKENS), I32),
            jax.ShapeDtypeStruct((TOP_K, TOKENS), F32),
            jax.ShapeDtypeStruct((N_EXPERTS, LANES), F32),
            jax.ShapeDtypeStruct((N_EXPERTS * EXPERT_CAP * ROW_TILE, LANES), F32),
        ],
        scratch_shapes=[
            pltpu.VMEM((2 * TM * ROW_TILE, LANES), F32),
            pltpu.VMEM((SUBLANES, TM), I32),
            pltpu.SMEM((SUBLANES, TM), I32),
            pltpu.VMEM((N_EXPERTS, LANES), I32),
            pltpu.SMEM((N_EXPERTS, LANES), I32),
            pltpu.VMEM((ROW_TILE, LANES), F32),
            pltpu.VMEM((N_EXPERTS, LANES), F32),
            pltpu.SemaphoreType.DMA(()),
            pltpu.SemaphoreType.DMA((2,)),
            pltpu.SemaphoreType.DMA(()),
        ],
        compiler_params=pltpu.CompilerParams(
            dimension_semantics=("arbitrary",), vmem_limit_bytes=VMEM_LIMIT),
        name="outproj_router",
    )(yc, ya, x, mod3, w_top, w_bot, mod3, mod3, norm_g, rw_hi, rw_lo, rb, tri)


def _expert_kernel(be_ref, br_ref, nu_ref, x_ref, w1_ref, b1_ref, w2_ref, b2_ref, o_ref, w1b, w2b):
    i = pl.program_id(0)

    @pl.when(i < nu_ref[0])
    def _():
        prev_e = be_ref[jnp.maximum(i - 1, 0)]

        @pl.when(jnp.logical_or(i == 0, be_ref[i] != prev_e))
        def _():
            w1b[...] = w1_ref[0, 0].astype(BF16)
            w2b[...] = w2_ref[0, 0].astype(BF16)

        xs = jnp.concatenate(
            [_load_row_tile_chunk(x_ref, 0, MOE_M, c) for c in range(ROW_TILE)], axis=1).astype(BF16)
        u = jnp.dot(xs, w1b[...], preferred_element_type=F32) + b1_ref[0, 0]
        glu = jnp.minimum(u[:, :D_EXPERT], SWIGLU_LIMIT)
        lin = jnp.clip(u[:, D_EXPERT:], -SWIGLU_LIMIT, SWIGLU_LIMIT)
        act = glu * _sigmoid(SWIGLU_ALPHA * glu) * (lin + 1.0)
        y = jnp.dot(act.astype(BF16), w2b[...], preferred_element_type=F32) + b2_ref[0, 0]
        _store_row_tiles(o_ref, 0, MOE_M, y)


def _experts(layer, block_exp, block_row, n_used, xs, w1, b1, w2, b2):
    wmap = lambda i, be, br, nu: (layer, be[i], 0, 0)
    rmap = lambda i, be, br, nu: (br[i], 0)
    grid_spec = pltpu.PrefetchScalarGridSpec(
        num_scalar_prefetch=3,
        grid=(N_BLOCKS,),
        in_specs=[
            pl.BlockSpec((MOE_M * ROW_TILE, LANES), rmap),
            pl.BlockSpec((1, 1, D_MODEL, 2 * D_EXPERT), wmap),
            pl.BlockSpec((1, 1, 1, 2 * D_EXPERT), wmap),
            pl.BlockSpec((1, 1, D_EXPERT, D_MODEL), wmap),
            pl.BlockSpec((1, 1, 1, D_MODEL), wmap),
        ],
        out_specs=pl.BlockSpec((MOE_M * ROW_TILE, LANES), rmap),
        scratch_shapes=[
            pltpu.VMEM((D_MODEL, 2 * D_EXPERT), BF16),
            pltpu.VMEM((D_EXPERT, D_MODEL), BF16),
        ],
    )
    return pl.pallas_call(
        _expert_kernel,
        grid_spec=grid_spec,
        out_shape=jax.ShapeDtypeStruct((N_EXPERTS * EXPERT_CAP * ROW_TILE, LANES), F32),
        compiler_params=pltpu.CompilerParams(
            dimension_semantics=("arbitrary",), vmem_limit_bytes=VMEM_LIMIT),
        name="experts",
    )(block_exp, block_row, n_used, xs,
      w1, b1.reshape(DEPTH, N_EXPERTS, 1, 2 * D_EXPERT), w2,
      b2.reshape(DEPTH, N_EXPERTS, 1, D_MODEL))


def _block_tables(counts):
    nblk = (counts + MOE_M - 1) // MOE_M
    blk_end = jnp.cumsum(nblk)
    blk_start = blk_end - nblk
    n_used = blk_end[-1]
    step = jnp.minimum(jnp.arange(N_BLOCKS, dtype=I32), n_used - 1)
    block_exp = jnp.minimum(
        jnp.sum((step[:, None] >= blk_end[None, :]).astype(I32), axis=1), N_EXPERTS - 1)
    block_row = block_exp * CAP_BLOCKS + (step - blk_start[block_exp])
    return block_exp, block_row.astype(I32), n_used.astype(I32).reshape(1)


def _combine_kernel(idx0_ref, idxn_ref, ys_hbm, x1_ref, g2_ref, gate_ref, o_ref, buf, sem):
    i = pl.program_id(0)
    n = pl.num_programs(0)
    rows = TMC * TOP_K
    slot = i % 2

    def start_gather(idx_ref, s):
        def body(r, c):
            for u in range(DMA_UNROLL):
                j = r * DMA_UNROLL + u
                _tile_copy(ys_hbm, idx_ref[0, 0, j], buf, s * rows + j, sem.at[s]).start(priority=u % 2)
            return c

        lax.fori_loop(0, rows // DMA_UNROLL, body, 0)

    @pl.when(i == 0)
    def _():
        start_gather(idx0_ref, 0)

    @pl.when(i + 1 < n)
    def _():
        start_gather(idxn_ref, 1 - slot)

    base = pl.multiple_of(slot * (rows * ROW_TILE), rows * ROW_TILE)
    pltpu.make_async_copy(ys_hbm.at[pl.ds(0, rows * ROW_TILE), :],
                          buf.at[pl.ds(base, rows * ROW_TILE), :], sem.at[slot]).wait()
    gates = gate_ref[...]
    g2 = g2_ref[0]
    for c in range(ROW_TILE):
        cs = slice(c * LANES, (c + 1) * LANES)
        acc = jnp.zeros((TMC, LANES), F32)
        for kk in range(TOP_K):
            acc = acc + gates[:, kk:kk + 1] * _load_row_tile_chunk(
                buf, base + kk * TMC * ROW_TILE, TMC, c)
        o_ref[:, cs] = x1_ref[:, cs] + g2[:, cs] * acc


def _combine(dest3, ys, x1, mod3, gates):
    n_tiles = TOKENS // TMC
    tiles_per_seq = SEQ // TMC
    rows = TMC * TOP_K
    return pl.pallas_call(
        _combine_kernel,
        grid=(n_tiles,),
        in_specs=[
            pl.BlockSpec((1, 1, rows), lambda i: (0, 0, 0), memory_space=pltpu.SMEM),
            pl.BlockSpec((1, 1, rows), lambda i: (jnp.minimum(i + 1, n_tiles - 1), 0, 0),
                         memory_space=pltpu.SMEM),
            pl.BlockSpec(memory_space=pl.ANY),
            pl.BlockSpec((TMC, D_MODEL), lambda i: (i, 0)),
            pl.BlockSpec((1, 1, D_MODEL), lambda i: (i // tiles_per_seq, 0, 5)),
            pl.BlockSpec((TMC, TOP_K), lambda i: (i, 0)),
        ],
        out_specs=pl.BlockSpec((TMC, D_MODEL), lambda i: (i, 0)),
        out_shape=jax.ShapeDtypeStruct((TOKENS, D_MODEL), F32),
        scratch_shapes=[
            pltpu.VMEM((2 * rows * ROW_TILE, LANES), F32),
            pltpu.SemaphoreType.DMA((2,)),
        ],
        compiler_params=pltpu.CompilerParams(
            dimension_semantics=("arbitrary",), vmem_limit_bytes=VMEM_LIMIT),
        name="moe_combine",
    )(dest3, dest3, ys, x1, mod3, gates)


_I = np.arange(Q_PAIR)[:, None]
_K = np.arange(KEY_WIN)[None, :]
_BAND_OK = (_K // CHUNK >= _I // CHUNK) & (_K // CHUNK <= _I // CHUNK + LEFT_CHUNKS)
_SEG = np.kron(np.eye(N_HEADS), np.full((HEAD_DIM, HEAD_DIM), 1.0 / HEAD_DIM)).astype(np.float32)
_TRI = (np.arange(TM)[:, None] < np.arange(TM)[None, :]).astype(np.float32)


def _bias_table(rel_bias_l):
    n_far = KEY_WIN - 1 + Q_PAIR - 2 * REL_CLIP
    ext = jnp.concatenate(
        [rel_bias_l[:, 1:], jnp.repeat(rel_bias_l[:, 2 * REL_CLIP:], n_far, axis=1)], axis=1)
    rev = ext[:, ::-1]
    rows = [rev[:, Q_PAIR - 1 - i:Q_PAIR - 1 - i + KEY_WIN] for i in range(Q_PAIR)]
    table = jnp.where(_BAND_OK[None], jnp.stack(rows, axis=1), NEG_INF).astype(F32)
    return table.reshape(N_HEADS // 2, 2, Q_PAIR, KEY_WIN).transpose(0, 3, 1, 2).reshape(
        N_HEADS // 2, KEY_WIN, 2 * Q_PAIR)


def kernel(x, c, ada_w, ada_b, norm1_g, w_in, conv_w, conv_b, conv_ln_g, conv_ln_b, q_norm_g,
           k_norm_g, rel_bias, w_out, norm2_g, router_w, router_b, exp_w1, exp_b1, exp_w2, exp_b2):
    xf = x.reshape(TOKENS, D_MODEL)
    mod = _ada_mod(c, ada_w, ada_b)
    seg = jnp.asarray(_SEG, BF16)
    tri = jnp.asarray(_TRI, BF16)
    pad_e = ((0, 0), (0, LANES - N_EXPERTS))
    for l in range(DEPTH):
        mod3 = mod[l].reshape(BATCH, 1, 6 * D_MODEL)
        qg = (jnp.tile(q_norm_g[l], N_HEADS) * (HEAD_DIM ** -0.5)).reshape(1, D_ATTN)
        kg = jnp.tile(k_norm_g[l], N_HEADS).reshape(1, D_ATTN)
        ug, q, k, v = _inproj(xf, mod3, norm1_g[l].reshape(1, D_MODEL), w_in[l].astype(BF16),
                              seg, qg, kg)
        y_conv = _conv(ug, conv_w[l], conv_b[l].reshape(1, D_CONV),
                       conv_ln_g[l].reshape(1, D_CONV), conv_ln_b[l].reshape(1, D_CONV))
        y_attn = _attention(q, k, v, _bias_table(rel_bias[l]))
        w_out_bf = w_out[l].astype(BF16)
        rw_hi = router_w[l].astype(BF16)
        rw_lo = (router_w[l] - rw_hi.astype(F32)).astype(BF16)
        x1, dest, gates, cnt, xs = _outproj(
            y_conv, y_attn, xf, mod3, w_out_bf[:D_CONV], w_out_bf[D_CONV:],
            norm2_g[l].reshape(1, D_MODEL), jnp.pad(rw_hi, pad_e), jnp.pad(rw_lo, pad_e),
            jnp.pad(router_b[l].reshape(1, N_EXPERTS), pad_e), tri)
        block_exp, block_row, n_used = _block_tables(cnt[:, 0].astype(I32))
        ys = _experts(l, block_exp, block_row, n_used, xs, exp_w1, exp_b1, exp_w2, exp_b2)
        dest3 = dest.reshape(TOP_K, TOKENS // TMC, TMC).transpose(1, 0, 2).reshape(
            TOKENS // TMC, 1, TMC * TOP_K)
        xf = _combine(dest3, ys, x1, mod3, gates.T)
    return xf.reshape(BATCH, SEQ, D_MODEL)
```

```python
import jax
import jax.numpy as jnp
import numpy as np
from jax import lax
from jax.experimental import pallas as pl
from jax.experimental.pallas import tpu as pltpu

D_MODEL = 1024
BATCH = 8
SEQ = 2048
DEPTH = 4
TOKENS = BATCH * SEQ

CHUNK = 64
D_CONV = D_MODEL // 2
D_ATTN = D_MODEL - D_CONV
N_HEADS = 8
HEAD_DIM = D_ATTN // N_HEADS
CONV_KERNEL = 31
LEFT_CHUNKS = 8
REL_CLIP = 128
N_EXPERTS = 32
TOP_K = 4
D_EXPERT = D_MODEL
SWIGLU_ALPHA = 1.702
SWIGLU_LIMIT = 7.0
EPS = 1e-6
D_IN = 2 * D_CONV + 3 * D_ATTN

F32 = jnp.float32
BF16 = jnp.bfloat16
I32 = jnp.int32
NEG_INF = float("-inf")

LANES = 128
SUBLANES = 8
TM = 512
TILES_PER_SEQ = SEQ // TM
Q_PAIR = 2 * CHUNK
KEY_WIN = (LEFT_CHUNKS + 2) * CHUNK
PAIRS_PER_TILE = TM // Q_PAIR
HALO = 32
CONV_ROWS = 32
MOE_M = 256
EXPERT_CAP = TOKENS
ROW_TILE = D_MODEL // LANES
TMC = 256
DMA_UNROLL = 8

VMEM_LIMIT = 56 * 1024 * 1024


def _sigmoid(x):
    return 1.0 / (1.0 + jnp.exp(-x))


def _ada_kernel(c_ref, w_ref, b_ref, o_ref):
    c = c_ref[...]
    ca = (c * _sigmoid(c)).astype(BF16)
    o_ref[0] = jnp.dot(ca, w_ref[0].astype(BF16), preferred_element_type=F32) + b_ref[0]


def _ada_mod(c, ada_w, ada_b):
    n_tile = 1536
    return pl.pallas_call(
        _ada_kernel,
        grid=(DEPTH, 6 * D_MODEL // n_tile),
        in_specs=[
            pl.BlockSpec((BATCH, D_MODEL), lambda l, n: (0, 0)),
            pl.BlockSpec((1, D_MODEL, n_tile), lambda l, n: (l, 0, n)),
            pl.BlockSpec((1, 1, n_tile), lambda l, n: (l, 0, n)),
        ],
        out_specs=pl.BlockSpec((1, BATCH, n_tile), lambda l, n: (l, 0, n)),
        out_shape=jax.ShapeDtypeStruct((DEPTH, BATCH, 6 * D_MODEL), F32),
        compiler_params=pltpu.CompilerParams(
            dimension_semantics=("arbitrary", "arbitrary"), vmem_limit_bytes=VMEM_LIMIT),
        name="ada_mod",
    )(c, ada_w, ada_b.reshape(DEPTH, 1, 6 * D_MODEL))


def _modulated_norm(x, g, sc, sh):
    ms = jnp.mean(x * x, axis=-1, keepdims=True)
    return x * lax.rsqrt(ms + EPS) * g * (1.0 + sc) + sh


def _inproj_kernel(x_ref, *refs):
    _inproj_body(x_ref[...], *refs)


def _inproj_body(x, sh_ref, sc_ref, g_ref, w_ref, seg_ref, qg_ref, kg_ref, ug_ref, q_ref, k_ref, v_ref):
    h = _modulated_norm(x, g_ref[...], sc_ref[0], sh_ref[0])
    u = jnp.dot(h.astype(BF16), w_ref[...], preferred_element_type=F32)
    a = u[:, :D_CONV]
    gt = u[:, D_CONV:2 * D_CONV]
    ug_ref[...] = (a * _sigmoid(gt)).astype(BF16)
    o = 2 * D_CONV
    q = u[:, o:o + D_ATTN]
    k = u[:, o + D_ATTN:o + 2 * D_ATTN]
    v = u[:, o + 2 * D_ATTN:]
    qms = jnp.dot((q * q).astype(BF16), seg_ref[...], preferred_element_type=F32)
    kms = jnp.dot((k * k).astype(BF16), seg_ref[...], preferred_element_type=F32)
    q_ref[...] = (q * lax.rsqrt(qms + EPS) * qg_ref[...]).astype(BF16)
    k_ref[...] = (k * lax.rsqrt(kms + EPS) * kg_ref[...]).astype(BF16)
    v_ref[...] = v.astype(BF16)


def _inproj(x, mod3, norm_g, w_in_bf, seg, qg, kg):
    row = lambda i: (i, 0)
    const = lambda i: (0, 0)
    out_sd = jax.ShapeDtypeStruct((TOKENS, D_CONV), BF16)
    return pl.pallas_call(
        _inproj_kernel,
        grid=(TOKENS // TM,),
        in_specs=[
            pl.BlockSpec((TM, D_MODEL), row),
            pl.BlockSpec((1, 1, D_MODEL), lambda i: (i // TILES_PER_SEQ, 0, 0)),
            pl.BlockSpec((1, 1, D_MODEL), lambda i: (i // TILES_PER_SEQ, 0, 1)),
            pl.BlockSpec((1, D_MODEL), const),
            pl.BlockSpec((D_MODEL, D_IN), const),
            pl.BlockSpec((D_ATTN, D_ATTN), const),
            pl.BlockSpec((1, D_ATTN), const),
            pl.BlockSpec((1, D_ATTN), const),
        ],
        out_specs=[pl.BlockSpec((TM, D_CONV), row)] * 4,
        out_shape=[out_sd] * 4,
        compiler_params=pltpu.CompilerParams(
            dimension_semantics=("arbitrary",), vmem_limit_bytes=VMEM_LIMIT),
        name="inproj",
    )(x, mod3, mod3, norm_g, w_in_bf, seg, qg, kg)


def _conv_kernel(cur_ref, halo_ref, w_ref, b_ref, lg_ref, lb_ref, o_ref, xe_ref):
    i = pl.program_id(0)
    seq_start = (i % TILES_PER_SEQ) == 0
    halo = halo_ref[...].astype(F32)
    xe_ref[0, 0:HALO, :] = jnp.where(seq_start, 0.0, halo)
    xe_ref[0, HALO:, :] = cur_ref[...].astype(F32)
    ext = HALO + TM
    for s in range(1, SUBLANES):
        xe_ref[s, 0:ext - SUBLANES, :] = xe_ref[0, s:ext - SUBLANES + s, :]
    n_lane = D_CONV // LANES
    first_tap = HALO - (CONV_KERNEL - 1)

    def body(rc, carry):
        r0 = pl.multiple_of(rc * CONV_ROWS, CONV_ROWS)
        lane_slices = [slice(c * LANES, (c + 1) * LANES) for c in range(n_lane)]
        accs = [jnp.zeros((CONV_ROWS, LANES), F32) + b_ref[:, cs] for cs in lane_slices]
        for j in range(CONV_KERNEL):
            shift = (first_tap + j) % SUBLANES
            aligned = first_tap + j - shift
            for c, cs in enumerate(lane_slices):
                accs[c] = accs[c] + w_ref[j:j + 1, cs] * xe_ref[shift, pl.ds(r0 + aligned, CONV_ROWS), cs]
        tot = accs[0].sum(axis=-1, keepdims=True)
        for c in range(1, n_lane):
            tot = tot + accs[c].sum(axis=-1, keepdims=True)
        mu = tot * (1.0 / D_CONV)
        cen = [a - mu for a in accs]
        var = (cen[0] * cen[0]).sum(axis=-1, keepdims=True)
        for c in range(1, n_lane):
            var = var + (cen[c] * cen[c]).sum(axis=-1, keepdims=True)
        inv = lax.rsqrt(var * (1.0 / D_CONV) + EPS)
        for c in range(n_lane):
            cs = slice(c * LANES, (c + 1) * LANES)
            y = cen[c] * inv * lg_ref[:, cs] + lb_ref[:, cs]
            o_ref[pl.ds(r0, CONV_ROWS), cs] = (y * _sigmoid(y)).astype(BF16)
        return carry

    lax.fori_loop(0, TM // CONV_ROWS, body, 0)


def _conv(ug, conv_w, conv_b, ln_g, ln_b):
    const = lambda i: (0, 0)
    halo_per_tile = TM // HALO
    return pl.pallas_call(
        _conv_kernel,
        grid=(TOKENS // TM,),
        in_specs=[
            pl.BlockSpec((TM, D_CONV), lambda i: (i, 0)),
            pl.BlockSpec((HALO, D_CONV), lambda i: (jnp.maximum(i * halo_per_tile - 1, 0), 0)),
            pl.BlockSpec((CONV_KERNEL, D_CONV), const),
            pl.BlockSpec((1, D_CONV), const),
            pl.BlockSpec((1, D_CONV), const),
            pl.BlockSpec((1, D_CONV), const),
        ],
        out_specs=pl.BlockSpec((TM, D_CONV), lambda i: (i, 0)),
        out_shape=jax.ShapeDtypeStruct((TOKENS, D_CONV), BF16),
        scratch_shapes=[pltpu.VMEM((SUBLANES, HALO + TM, D_CONV), F32)],
        compiler_params=pltpu.CompilerParams(
            dimension_semantics=("arbitrary",), vmem_limit_bytes=VMEM_LIMIT),
        name="conv_mixer",
    )(ug, ug, conv_w, conv_b, ln_g, ln_b)


def _attn_kernel(q_ref, kp_ref, kc_ref, vp_ref, vc_ref, bias_ref, o_ref, kw_ref, vt_ref):
    i = pl.program_id(0)
    seq_start = (i % TILES_PER_SEQ) == 0
    kw_ref[0:TM, :] = kp_ref[...]
    kw_ref[TM:, :] = kc_ref[...]
    vt_ref[:, 0:TM] = vp_ref[...].T
    vt_ref[:, TM:] = vc_ref[...].T
    pair = 2 * HEAD_DIM
    low = lax.broadcasted_iota(jnp.int32, (Q_PAIR, pair), 1) < HEAD_DIM

    def pair_block(cp, hp, k_lo):
        r0 = cp * Q_PAIR
        ps = slice(hp * pair, (hp + 1) * pair)
        q2 = q_ref[r0:r0 + Q_PAIR, ps]
        zero = jnp.zeros_like(q2)
        qm = jnp.concatenate([jnp.where(low, q2, zero), jnp.where(low, zero, q2)], axis=0)
        k2 = kw_ref[r0 + k_lo:r0 + KEY_WIN, ps]
        st = lax.dot_general(k2, qm, (((1,), (1,)), ((), ())), preferred_element_type=F32)
        st = st + bias_ref[hp, k_lo:, :]
        m = jnp.max(st, axis=0, keepdims=True)
        e = jnp.exp(st - m)
        l = jnp.sum(e, axis=0, keepdims=True)
        ot = jnp.dot(vt_ref[ps, r0 + k_lo:r0 + KEY_WIN], e.astype(BF16),
                     preferred_element_type=F32) * (1.0 / l)
        o2t = jnp.concatenate([ot[:HEAD_DIM, :Q_PAIR], ot[HEAD_DIM:, Q_PAIR:]], axis=0)
        o_ref[r0:r0 + Q_PAIR, ps] = o2t.T.astype(BF16)

    @pl.when(seq_start)
    def _():
        for cp in range(PAIRS_PER_TILE):
            for hp in range(N_HEADS // 2):
                pair_block(cp, hp, TM - cp * Q_PAIR)

    @pl.when(jnp.logical_not(seq_start))
    def _():
        for cp in range(PAIRS_PER_TILE):
            for hp in range(N_HEADS // 2):
                pair_block(cp, hp, 0)


def _attention(q, k, v, bias2):
    cur = lambda i: (i, 0)
    prev = lambda i: (jnp.maximum(i - 1, 0), 0)
    blk = (TM, D_ATTN)
    return pl.pallas_call(
        _attn_kernel,
        grid=(TOKENS // TM,),
        in_specs=[
            pl.BlockSpec(blk, cur),
            pl.BlockSpec(blk, prev), pl.BlockSpec(blk, cur),
            pl.BlockSpec(blk, prev), pl.BlockSpec(blk, cur),
            pl.BlockSpec((N_HEADS // 2, KEY_WIN, 2 * Q_PAIR), lambda i: (0, 0, 0)),
        ],
        out_specs=pl.BlockSpec(blk, cur),
        out_shape=jax.ShapeDtypeStruct((TOKENS, D_ATTN), BF16),
        scratch_shapes=[pltpu.VMEM((2 * TM, D_ATTN), BF16), pltpu.VMEM((D_ATTN, 2 * TM), BF16)],
        compiler_params=pltpu.CompilerParams(
            dimension_semantics=("arbitrary",), vmem_limit_bytes=VMEM_LIMIT),
        name="chunk_attention",
    )(q, k, k, v, v, bias2)


def _store_row_tiles(ref, base, rows, value):
    for c in range(ROW_TILE):
        ref[pl.ds(base + c, rows, stride=ROW_TILE), :] = value[:, c * LANES:(c + 1) * LANES]


def _load_row_tile_chunk(ref, base, rows, c):
    return ref[pl.ds(base + c, rows, stride=ROW_TILE), :]


def _tile_copy(src_ref, src_row, dst_ref, dst_row, sem):
    return pltpu.make_async_copy(
        src_ref.at[pl.ds(src_row * ROW_TILE, ROW_TILE), :],
        dst_ref.at[pl.ds(dst_row * ROW_TILE, ROW_TILE), :], sem)


def _outproj_kernel(yc_ref, ya_ref, x_ref, g1_ref, wt_ref, wb_ref, sh_ref, sc_ref, g_ref,
                    rwh_ref, rwl_ref, rb_ref, tri_ref,
                    x1_ref, dest_ref, gate_ref, cnt_ref, xs_hbm,
                    hbuf, dvm, dsm, cvm, csm, zbuf, carry, sem_idx, sem_push, sem_zero):
    i = pl.program_id(0)
    n = pl.num_programs(0)
    slot = i % 2

    @pl.when(i == 0)
    def _():
        carry[...] = jnp.zeros_like(carry)

    mixed = (jnp.dot(yc_ref[...], wt_ref[...], preferred_element_type=F32)
             + jnp.dot(ya_ref[...], wb_ref[...], preferred_element_type=F32))
    x1 = x_ref[...] + g1_ref[0] * mixed
    x1_ref[...] = x1
    h2 = _modulated_norm(x1, g_ref[...], sc_ref[0], sh_ref[0])

    hi = h2.astype(BF16)
    lo = (h2 - hi.astype(F32)).astype(BF16)
    logits = (jnp.dot(hi, rwh_ref[...], preferred_element_type=F32)
              + jnp.dot(hi, rwl_ref[...], preferred_element_type=F32)
              + jnp.dot(lo, rwh_ref[...], preferred_element_type=F32)) + rb_ref[...]
    lt = logits.T[:N_EXPERTS, :]

    eid = lax.broadcasted_iota(I32, (N_EXPERTS, TM), 0)
    picks, onehots, exps = [], [], []
    top0 = None
    for kk in range(TOP_K):
        m = jnp.max(lt, axis=0, keepdims=True)
        am = jnp.min(jnp.where(lt == m, eid, N_EXPERTS), axis=0, keepdims=True)
        oh = eid == am
        lt = jnp.where(oh, NEG_INF, lt)
        if kk == 0:
            top0 = m
        picks.append(am)
        onehots.append(oh)
        exps.append(jnp.exp(m - top0))
    denom = exps[0] + exps[1] + exps[2] + exps[3]
    gate_ref[...] = jnp.concatenate(exps, axis=0) * (1.0 / denom)

    oh_all = (onehots[0].astype(F32) + onehots[1].astype(F32)
              + onehots[2].astype(F32) + onehots[3].astype(F32))
    before = jnp.dot(oh_all.astype(BF16), tri_ref[...], preferred_element_type=F32) + carry[:, 0:1]
    dests = []
    for kk in range(TOP_K):
        rank = jnp.sum(jnp.where(onehots[kk], before, 0.0), axis=0, keepdims=True)
        dests.append(picks[kk] * EXPERT_CAP + rank.astype(I32))
    dest = jnp.concatenate(dests, axis=0)
    dest_ref[...] = dest
    carry[...] = carry[...] + jnp.sum(oh_all, axis=1, keepdims=True)
    cnt_ref[...] = carry[...]

    dvm[...] = jnp.concatenate([dest, jnp.zeros((SUBLANES - TOP_K, TM), I32)], axis=0)
    idx_copy = pltpu.make_async_copy(dvm, dsm, sem_idx)
    idx_copy.start()

    hb = pl.multiple_of(slot * (TM * ROW_TILE), TM * ROW_TILE)
    _store_row_tiles(hbuf, hb, TM, h2)

    def wait_push(s):
        base = pl.multiple_of(s * (TM * ROW_TILE), TM * ROW_TILE)
        for _ in range(TOP_K):
            pltpu.make_async_copy(hbuf.at[pl.ds(base, TM * ROW_TILE), :],
                                  xs_hbm.at[pl.ds(0, TM * ROW_TILE), :], sem_push.at[s]).wait()

    @pl.when(i > 0)
    def _():
        wait_push(1 - slot)

    idx_copy.wait()
    per_iter = DMA_UNROLL // TOP_K

    def push_body(r, c):
        for u in range(DMA_UNROLL):
            t = r * per_iter + u // TOP_K
            d = dsm[u % TOP_K, t]
            _tile_copy(hbuf, slot * TM + t, xs_hbm, d, sem_push.at[slot]).start(priority=u % 2)
        return c

    lax.fori_loop(0, TM // per_iter, push_body, 0)

    @pl.when(i == n - 1)
    def _():
        wait_push(slot)
        zbuf[...] = jnp.zeros_like(zbuf)
        cvm[...] = carry[...].astype(I32)
        cnt_copy = pltpu.make_async_copy(cvm, csm, sem_idx)
        cnt_copy.start()
        cnt_copy.wait()

        def per_expert(e, c, wait):
            cnt = csm[e, 0]
            end = (cnt + MOE_M - 1) // MOE_M * MOE_M

            def row(r, c2):
                cp = _tile_copy(zbuf, 0, xs_hbm, e * EXPERT_CAP + r, sem_zero)
                if wait:
                    cp.wait()
                else:
                    cp.start()
                return c2

            lax.fori_loop(cnt, end, row, 0)
            return c

        lax.fori_loop(0, N_EXPERTS, lambda e, c: per_expert(e, c, False), 0)
        lax.fori_loop(0, N_EXPERTS, lambda e, c: per_expert(e, c, True), 0)


def _outproj(yc, ya, x, mod3, w_top, w_bot, norm_g, rw_hi, rw_lo, rb, tri):
    row = lambda i: (i, 0)
    col = lambda i: (0, i)
    const = lambda i: (0, 0)
    modspec = lambda j: pl.BlockSpec((1, 1, D_MODEL), lambda i: (i // TILES_PER_SEQ, 0, j))
    return pl.pallas_call(
        _outproj_kernel,
        grid=(TOKENS // TM,),
        in_specs=[
            pl.BlockSpec((TM, D_CONV), row),
            pl.BlockSpec((TM, D_ATTN), row),
            pl.BlockSpec((TM, D_MODEL), row),
            modspec(2),
            pl.BlockSpec((D_CONV, D_MODEL), const),
            pl.BlockSpec((D_ATTN, D_MODEL), const),
            modspec(3),
            modspec(4),
            pl.BlockSpec((1, D_MODEL), const),
            pl.BlockSpec((D_MODEL, LANES), const),
            pl.BlockSpec((D_MODEL, LANES), const),
            pl.BlockSpec((1, LANES), const),
            pl.BlockSpec((TM, TM), const),
        ],
        out_specs=[
            pl.BlockSpec((TM, D_MODEL), row),
            pl.BlockSpec((TOP_K, TM), col),
            pl.BlockSpec((TOP_K, TM), col),
            pl.BlockSpec((N_EXPERTS, LANES), const),
            pl.BlockSpec(memory_space=pl.ANY),
        ],
        out_shape=[
            jax.ShapeDtypeStruct((TOKENS, D_MODEL), F32),
            jax.ShapeDtypeStruct((TOP_K, TOKENS), I32),
            jax.ShapeDtypeStruct((TOP_K, TOKENS), F32),
            jax.ShapeDtypeStruct((N_EXPERTS, LANES), F32),
            jax.ShapeDtypeStruct((N_EXPERTS * EXPERT_CAP * ROW_TILE, LANES), F32),
        ],
        scratch_shapes=[
            pltpu.VMEM((2 * TM * ROW_TILE, LANES), F32),
            pltpu.VMEM((SUBLANES, TM), I32),
            pltpu.SMEM((SUBLANES, TM), I32),
            pltpu.VMEM((N_EXPERTS, LANES), I32),
            pltpu.SMEM((N_EXPERTS, LANES), I32),
            pltpu.VMEM((ROW_TILE, LANES), F32),
            pltpu.VMEM((N_EXPERTS, LANES), F32),
            pltpu.SemaphoreType.DMA(()),
            pltpu.SemaphoreType.DMA((2,)),
            pltpu.SemaphoreType.DMA(()),
        ],
        compiler_params=pltpu.CompilerParams(
            dimension_semantics=("arbitrary",), vmem_limit_bytes=VMEM_LIMIT),
        name="outproj_router",
    )(yc, ya, x, mod3, w_top, w_bot, mod3, mod3, norm_g, rw_hi, rw_lo, rb, tri)


def _expert_kernel(nblk_ref, xs_hbm, w1_ref, b1_ref, w2_ref, b2_ref, ys_hbm,
                   xin, yout, w1b, w2b, sem_in, sem_out):
    e = pl.program_id(0)
    nb = nblk_ref[e]
    blk = MOE_M * ROW_TILE
    first = e * (EXPERT_CAP * ROW_TILE)

    def in_copy(j, s):
        return pltpu.make_async_copy(
            xs_hbm.at[pl.ds(pl.multiple_of(first + j * blk, blk), blk), :],
            xin.at[pl.ds(pl.multiple_of(s * blk, blk), blk), :], sem_in.at[s])

    def out_copy(j, s):
        return pltpu.make_async_copy(
            yout.at[pl.ds(pl.multiple_of(s * blk, blk), blk), :],
            ys_hbm.at[pl.ds(pl.multiple_of(first + j * blk, blk), blk), :], sem_out.at[s])

    @pl.when(nb > 0)
    def _():
        in_copy(0, 0).start()
        w1b[...] = w1_ref[0, 0].astype(BF16)
        w2b[...] = w2_ref[0, 0].astype(BF16)

        def body(j, carry):
            s = j % 2
            base = pl.multiple_of(s * blk, blk)

            @pl.when(j + 1 < nb)
            def _():
                in_copy(j + 1, 1 - s).start()

            in_copy(j, s).wait()
            xs = jnp.concatenate(
                [_load_row_tile_chunk(xin, base, MOE_M, c) for c in range(ROW_TILE)], axis=1).astype(BF16)
            u = jnp.dot(xs, w1b[...], preferred_element_type=F32) + b1_ref[0, 0]
            glu = jnp.minimum(u[:, :D_EXPERT], SWIGLU_LIMIT)
            lin = jnp.clip(u[:, D_EXPERT:], -SWIGLU_LIMIT, SWIGLU_LIMIT)
            act = glu * _sigmoid(SWIGLU_ALPHA * glu) * (lin + 1.0)
            y = jnp.dot(act.astype(BF16), w2b[...], preferred_element_type=F32) + b2_ref[0, 0]

            @pl.when(j >= 2)
            def _():
                out_copy(j - 2, s).wait()

            _store_row_tiles(yout, base, MOE_M, y)
            out_copy(j, s).start()
            return carry

        lax.fori_loop(0, nb, body, 0)

        @pl.when(nb >= 2)
        def _():
            out_copy(nb - 2, nb % 2).wait()

        out_copy(nb - 1, (nb - 1) % 2).wait()


def _experts(layer, nblk, xs, w1, b1, w2, b2):
    wmap = lambda e, nb: (layer, e, 0, 0)
    blk = MOE_M * ROW_TILE
    grid_spec = pltpu.PrefetchScalarGridSpec(
        num_scalar_prefetch=1,
        grid=(N_EXPERTS,),
        in_specs=[
            pl.BlockSpec(memory_space=pl.ANY),
            pl.BlockSpec((1, 1, D_MODEL, 2 * D_EXPERT), wmap),
            pl.BlockSpec((1, 1, 1, 2 * D_EXPERT), wmap),
            pl.BlockSpec((1, 1, D_EXPERT, D_MODEL), wmap),
            pl.BlockSpec((1, 1, 1, D_MODEL), wmap),
        ],
        out_specs=pl.BlockSpec(memory_space=pl.ANY),
        scratch_shapes=[
            pltpu.VMEM((2 * blk, LANES), F32),
            pltpu.VMEM((2 * blk, LANES), F32),
            pltpu.VMEM((D_MODEL, 2 * D_EXPERT), BF16),
            pltpu.VMEM((D_EXPERT, D_MODEL), BF16),
            pltpu.SemaphoreType.DMA((2,)),
            pltpu.SemaphoreType.DMA((2,)),
        ],
    )
    return pl.pallas_call(
        _expert_kernel,
        grid_spec=grid_spec,
        out_shape=jax.ShapeDtypeStruct((N_EXPERTS * EXPERT_CAP * ROW_TILE, LANES), F32),
        compiler_params=pltpu.CompilerParams(
            dimension_semantics=("arbitrary",), vmem_limit_bytes=VMEM_LIMIT),
        name="experts",
    )(nblk, xs, w1, b1.reshape(DEPTH, N_EXPERTS, 1, 2 * D_EXPERT), w2,
      b2.reshape(DEPTH, N_EXPERTS, 1, D_MODEL))


def _gather_combine(tile, idx0_ref, idxn_ref, ys_hbm, x1_ref, g2_ref, gate_ref, o_ref, buf, sem):
    i = pl.program_id(0)
    n = pl.num_programs(0)
    rows = tile * TOP_K
    slot = i % 2

    def start_gather(idx_ref, s):
        def body(r, c):
            for u in range(DMA_UNROLL):
                j = r * DMA_UNROLL + u
                _tile_copy(ys_hbm, idx_ref[0, 0, j], buf, s * rows + j, sem.at[s]).start(priority=u % 2)
            return c

        lax.fori_loop(0, rows // DMA_UNROLL, body, 0)

    @pl.when(i == 0)
    def _():
        start_gather(idx0_ref, 0)

    @pl.when(i + 1 < n)
    def _():
        start_gather(idxn_ref, 1 - slot)

    base = pl.multiple_of(slot * (rows * ROW_TILE), rows * ROW_TILE)
    pltpu.make_async_copy(ys_hbm.at[pl.ds(0, rows * ROW_TILE), :],
                          buf.at[pl.ds(base, rows * ROW_TILE), :], sem.at[slot]).wait()
    gates = gate_ref[...]
    g2 = g2_ref[0]
    for c in range(ROW_TILE):
        cs = slice(c * LANES, (c + 1) * LANES)
        acc = jnp.zeros((tile, LANES), F32)
        for kk in range(TOP_K):
            acc = acc + gates[:, kk:kk + 1] * _load_row_tile_chunk(
                buf, base + kk * tile * ROW_TILE, tile, c)
        o_ref[:, cs] = x1_ref[:, cs] + g2[:, cs] * acc


def _combine_kernel(*refs):
    _gather_combine(TMC, *refs)


def _combine_inproj_kernel(idx0_ref, idxn_ref, ys_hbm, x1_ref, g2_ref, gate_ref,
                           sh_ref, sc_ref, g_ref, w_ref, seg_ref, qg_ref, kg_ref,
                           xn_ref, ug_ref, q_ref, k_ref, v_ref, buf, sem):
    _gather_combine(TM, idx0_ref, idxn_ref, ys_hbm, x1_ref, g2_ref, gate_ref, xn_ref, buf, sem)
    _inproj_body(xn_ref[...], sh_ref, sc_ref, g_ref, w_ref, seg_ref, qg_ref, kg_ref,
                 ug_ref, q_ref, k_ref, v_ref)


def _choice_major(dest, tile):
    return dest.reshape(TOP_K, TOKENS // tile, tile).transpose(1, 0, 2).reshape(
        TOKENS // tile, 1, TOP_K * tile)


def _combine_inproj(dest, ys, x1, mod3_prev, gates, mod3, norm_g, w_in_bf, seg, qg, kg):
    n_tiles = TOKENS // TM
    rows = TM * TOP_K
    row = lambda i: (i, 0)
    const = lambda i: (0, 0)
    modspec = lambda j: pl.BlockSpec((1, 1, D_MODEL), lambda i: (i // TILES_PER_SEQ, 0, j))
    half_sd = jax.ShapeDtypeStruct((TOKENS, D_CONV), BF16)
    return pl.pallas_call(
        _combine_inproj_kernel,
        grid=(n_tiles,),
        in_specs=[
            pl.BlockSpec((1, 1, rows), lambda i: (0, 0, 0), memory_space=pltpu.SMEM),
            pl.BlockSpec((1, 1, rows), lambda i: (jnp.minimum(i + 1, n_tiles - 1), 0, 0),
                         memory_space=pltpu.SMEM),
            pl.BlockSpec(memory_space=pl.ANY),
            pl.BlockSpec((TM, D_MODEL), row),
            modspec(5),
            pl.BlockSpec((TM, TOP_K), row),
            modspec(0),
            modspec(1),
            pl.BlockSpec((1, D_MODEL), const),
            pl.BlockSpec((D_MODEL, D_IN), const),
            pl.BlockSpec((D_ATTN, D_ATTN), const),
            pl.BlockSpec((1, D_ATTN), const),
            pl.BlockSpec((1, D_ATTN), const),
        ],
        out_specs=[pl.BlockSpec((TM, D_MODEL), row)] + [pl.BlockSpec((TM, D_CONV), row)] * 4,
        out_shape=[jax.ShapeDtypeStruct((TOKENS, D_MODEL), F32)] + [half_sd] * 4,
        scratch_shapes=[
            pltpu.VMEM((2 * rows * ROW_TILE, LANES), F32),
            pltpu.SemaphoreType.DMA((2,)),
        ],
        compiler_params=pltpu.CompilerParams(
            dimension_semantics=("arbitrary",), vmem_limit_bytes=VMEM_LIMIT),
        name="combine_inproj",
    )(_choice_major(dest, TM), _choice_major(dest, TM), ys, x1, mod3_prev, gates,
      mod3, mod3, norm_g, w_in_bf, seg, qg, kg)


def _combine(dest3, ys, x1, mod3, gates):
    n_tiles = TOKENS // TMC
    tiles_per_seq = SEQ // TMC
    rows = TMC * TOP_K
    return pl.pallas_call(
        _combine_kernel,
        grid=(n_tiles,),
        in_specs=[
            pl.BlockSpec((1, 1, rows), lambda i: (0, 0, 0), memory_space=pltpu.SMEM),
            pl.BlockSpec((1, 1, rows), lambda i: (jnp.minimum(i + 1, n_tiles - 1), 0, 0),
                         memory_space=pltpu.SMEM),
            pl.BlockSpec(memory_space=pl.ANY),
            pl.BlockSpec((TMC, D_MODEL), lambda i: (i, 0)),
            pl.BlockSpec((1, 1, D_MODEL), lambda i: (i // tiles_per_seq, 0, 5)),
            pl.BlockSpec((TMC, TOP_K), lambda i: (i, 0)),
        ],
        out_specs=pl.BlockSpec((TMC, D_MODEL), lambda i: (i, 0)),
        out_shape=jax.ShapeDtypeStruct((TOKENS, D_MODEL), F32),
        scratch_shapes=[
            pltpu.VMEM((2 * rows * ROW_TILE, LANES), F32),
            pltpu.SemaphoreType.DMA((2,)),
        ],
        compiler_params=pltpu.CompilerParams(
            dimension_semantics=("arbitrary",), vmem_limit_bytes=VMEM_LIMIT),
        name="moe_combine",
    )(dest3, dest3, ys, x1, mod3, gates)


_I = np.arange(Q_PAIR)[:, None]
_K = np.arange(KEY_WIN)[None, :]
_BAND_OK = (_K // CHUNK >= _I // CHUNK) & (_K // CHUNK <= _I // CHUNK + LEFT_CHUNKS)
_SEG = np.kron(np.eye(N_HEADS), np.full((HEAD_DIM, HEAD_DIM), 1.0 / HEAD_DIM)).astype(np.float32)
_TRI = (np.arange(TM)[:, None] < np.arange(TM)[None, :]).astype(np.float32)


def _bias_tables(rel_bias):
    n_far = KEY_WIN - 1 + Q_PAIR - 2 * REL_CLIP
    ext = jnp.concatenate(
        [rel_bias[..., 1:], jnp.repeat(rel_bias[..., 2 * REL_CLIP:], n_far, axis=-1)], axis=-1)
    rev = ext[..., ::-1]
    period = KEY_WIN + Q_PAIR
    ring = jnp.concatenate(
        [rev[..., Q_PAIR - 1:], jnp.zeros(rev.shape[:-1] + (1,), rev.dtype), rev[..., :Q_PAIR - 1]],
        axis=-1)
    lead = rel_bias.shape[:-1]
    skew = jnp.tile(ring, Q_PAIR)[..., :Q_PAIR * (period - 1)].reshape(lead + (Q_PAIR, period - 1))
    table = jnp.where(_BAND_OK, skew[..., :KEY_WIN], NEG_INF).astype(F32)
    return table.reshape(DEPTH, N_HEADS // 2, 2, Q_PAIR, KEY_WIN).transpose(0, 1, 4, 2, 3).reshape(
        DEPTH, N_HEADS // 2, KEY_WIN, 2 * Q_PAIR)


def kernel(x, c, ada_w, ada_b, norm1_g, w_in, conv_w, conv_b, conv_ln_g, conv_ln_b, q_norm_g,
           k_norm_g, rel_bias, w_out, norm2_g, router_w, router_b, exp_w1, exp_b1, exp_w2, exp_b2):
    xf = x.reshape(TOKENS, D_MODEL)
    mod = _ada_mod(c, ada_w, ada_b)
    seg = jnp.asarray(_SEG, BF16)
    tri = jnp.asarray(_TRI, BF16)
    pad_e = ((0, 0), (0, LANES - N_EXPERTS))
    bias_tables = _bias_tables(rel_bias)
    moe = None
    for l in range(DEPTH):
        mod3 = mod[l].reshape(BATCH, 1, 6 * D_MODEL)
        qg = (jnp.tile(q_norm_g[l], N_HEADS) * (HEAD_DIM ** -0.5)).reshape(1, D_ATTN)
        kg = jnp.tile(k_norm_g[l], N_HEADS).reshape(1, D_ATTN)
        inproj_params = (mod3, norm1_g[l].reshape(1, D_MODEL), w_in[l].astype(BF16), seg, qg, kg)
        if moe is None:
            ug, q, k, v = _inproj(xf, *inproj_params)
        else:
            xf, ug, q, k, v = _combine_inproj(*moe, *inproj_params)
        y_conv = _conv(ug, conv_w[l], conv_b[l].reshape(1, D_CONV),
                       conv_ln_g[l].reshape(1, D_CONV), conv_ln_b[l].reshape(1, D_CONV))
        y_attn = _attention(q, k, v, bias_tables[l])
        w_out_bf = w_out[l].astype(BF16)
        rw_hi = router_w[l].astype(BF16)
        rw_lo = (router_w[l] - rw_hi.astype(F32)).astype(BF16)
        x1, dest, gates, cnt, xs = _outproj(
            y_conv, y_attn, xf, mod3, w_out_bf[:D_CONV], w_out_bf[D_CONV:],
            norm2_g[l].reshape(1, D_MODEL), jnp.pad(rw_hi, pad_e), jnp.pad(rw_lo, pad_e),
            jnp.pad(router_b[l].reshape(1, N_EXPERTS), pad_e), tri)
        nblk = (cnt[:, 0].astype(I32) + MOE_M - 1) // MOE_M
        ys = _experts(l, nblk, xs, exp_w1, exp_b1, exp_w2, exp_b2)
        moe = (dest, ys, x1, mod3, gates.T)
    dest, ys, x1, mod3, gates_t = moe
    xf = _combine(_choice_major(dest, TMC), ys, x1, mod3, gates_t)
    return xf.reshape(BATCH, SEQ, D_MODEL)
```

```python
import jax
import jax.numpy as jnp
import numpy as np
from jax import lax
from jax.experimental import pallas as pl
from jax.experimental.pallas import tpu as pltpu

D_MODEL = 1024
BATCH = 8
SEQ = 2048
DEPTH = 4
TOKENS = BATCH * SEQ

CHUNK = 64
D_CONV = D_MODEL // 2
D_ATTN = D_MODEL - D_CONV
N_HEADS = 8
HEAD_DIM = D_ATTN // N_HEADS
CONV_KERNEL = 31
LEFT_CHUNKS = 8
REL_CLIP = 128
N_EXPERTS = 32
TOP_K = 4
D_EXPERT = D_MODEL
SWIGLU_ALPHA = 1.702
SWIGLU_LIMIT = 7.0
EPS = 1e-6
D_IN = 2 * D_CONV + 3 * D_ATTN

F32 = jnp.float32
BF16 = jnp.bfloat16
I32 = jnp.int32
NEG_INF = float("-inf")

LANES = 128
SUBLANES = 8
TM = 512
TILES_PER_SEQ = SEQ // TM
Q_PAIR = 2 * CHUNK
KEY_WIN = (LEFT_CHUNKS + 2) * CHUNK
PAIRS_PER_TILE = TM // Q_PAIR
HALO = 32
CONV_ROWS = 32
CONV_CHUNKS_PER_ITER = 4
MOE_M = 256
EXPERT_CAP = TOKENS
ROW_TILE = D_MODEL // LANES
TMC = 256
DMA_UNROLL = 8

VMEM_LIMIT = 56 * 1024 * 1024


def _sigmoid(x):
    return 1.0 / (1.0 + jnp.exp(-x))


def _ada_kernel(c_ref, w_ref, b_ref, o_ref):
    c = c_ref[...]
    ca = (c * _sigmoid(c)).astype(BF16)
    o_ref[0] = jnp.dot(ca, w_ref[0].astype(BF16), preferred_element_type=F32) + b_ref[0]


def _ada_mod(c, ada_w, ada_b):
    n_tile = 1536
    return pl.pallas_call(
        _ada_kernel,
        grid=(DEPTH, 6 * D_MODEL // n_tile),
        in_specs=[
            pl.BlockSpec((BATCH, D_MODEL), lambda l, n: (0, 0)),
            pl.BlockSpec((1, D_MODEL, n_tile), lambda l, n: (l, 0, n)),
            pl.BlockSpec((1, 1, n_tile), lambda l, n: (l, 0, n)),
        ],
        out_specs=pl.BlockSpec((1, BATCH, n_tile), lambda l, n: (l, 0, n)),
        out_shape=jax.ShapeDtypeStruct((DEPTH, BATCH, 6 * D_MODEL), F32),
        compiler_params=pltpu.CompilerParams(
            dimension_semantics=("arbitrary", "arbitrary"), vmem_limit_bytes=VMEM_LIMIT),
        name="ada_mod",
    )(c, ada_w, ada_b.reshape(DEPTH, 1, 6 * D_MODEL))


def _modulated_norm(x, g, sc, sh):
    ms = jnp.mean(x * x, axis=-1, keepdims=True)
    return x * lax.rsqrt(ms + EPS) * g * (1.0 + sc) + sh


def _inproj_kernel(x_ref, *refs):
    _inproj_body(x_ref[...], *refs)


def _inproj_body(x, sh_ref, sc_ref, g_ref, w_ref, seg_ref, qg_ref, kg_ref, ug_ref, q_ref, k_ref, v_ref):
    h = _modulated_norm(x, g_ref[...], sc_ref[0], sh_ref[0])
    u = jnp.dot(h.astype(BF16), w_ref[...], preferred_element_type=F32)
    a = u[:, :D_CONV]
    gt = u[:, D_CONV:2 * D_CONV]
    ug_ref[...] = (a * _sigmoid(gt)).astype(BF16)
    o = 2 * D_CONV
    q = u[:, o:o + D_ATTN]
    k = u[:, o + D_ATTN:o + 2 * D_ATTN]
    v = u[:, o + 2 * D_ATTN:]
    qms = jnp.dot((q * q).astype(BF16), seg_ref[...], preferred_element_type=F32)
    kms = jnp.dot((k * k).astype(BF16), seg_ref[...], preferred_element_type=F32)
    q_ref[...] = (q * lax.rsqrt(qms + EPS) * qg_ref[...]).astype(BF16)
    k_ref[...] = (k * lax.rsqrt(kms + EPS) * kg_ref[...]).astype(BF16)
    v_ref[...] = v.astype(BF16)


def _inproj(x, mod3, norm_g, w_in_bf, seg, qg, kg):
    row = lambda i: (i, 0)
    const = lambda i: (0, 0)
    out_sd = jax.ShapeDtypeStruct((TOKENS, D_CONV), BF16)
    return pl.pallas_call(
        _inproj_kernel,
        grid=(TOKENS // TM,),
        in_specs=[
            pl.BlockSpec((TM, D_MODEL), row),
            pl.BlockSpec((1, 1, D_MODEL), lambda i: (i // TILES_PER_SEQ, 0, 0)),
            pl.BlockSpec((1, 1, D_MODEL), lambda i: (i // TILES_PER_SEQ, 0, 1)),
            pl.BlockSpec((1, D_MODEL), const),
            pl.BlockSpec((D_MODEL, D_IN), const),
            pl.BlockSpec((D_ATTN, D_ATTN), const),
            pl.BlockSpec((1, D_ATTN), const),
            pl.BlockSpec((1, D_ATTN), const),
        ],
        out_specs=[pl.BlockSpec((TM, D_CONV), row)] * 4,
        out_shape=[out_sd] * 4,
        compiler_params=pltpu.CompilerParams(
            dimension_semantics=("arbitrary",), vmem_limit_bytes=VMEM_LIMIT),
        name="inproj",
    )(x, mod3, mod3, norm_g, w_in_bf, seg, qg, kg)


def _conv_kernel(cur_ref, halo_ref, w_ref, b_ref, lg_ref, lb_ref, o_ref, xe_ref):
    i = pl.program_id(0)
    seq_start = (i % TILES_PER_SEQ) == 0
    halo = halo_ref[...].astype(F32)
    xe_ref[0, 0:HALO, :] = jnp.where(seq_start, 0.0, halo)
    xe_ref[0, HALO:, :] = cur_ref[...].astype(F32)
    ext = HALO + TM
    for s in range(1, SUBLANES):
        xe_ref[s, 0:ext - SUBLANES, :] = xe_ref[0, s:ext - SUBLANES + s, :]
    n_lane = D_CONV // LANES
    first_tap = HALO - (CONV_KERNEL - 1)

    lane_slices = [slice(c * LANES, (c + 1) * LANES) for c in range(n_lane)]

    def row_chunk(r0):
        accs = [jnp.zeros((CONV_ROWS, LANES), F32) + b_ref[:, cs] for cs in lane_slices]
        for j in range(CONV_KERNEL):
            shift = (first_tap + j) % SUBLANES
            aligned = first_tap + j - shift
            for c, cs in enumerate(lane_slices):
                accs[c] = accs[c] + w_ref[j:j + 1, cs] * xe_ref[shift, pl.ds(r0 + aligned, CONV_ROWS), cs]
        tot = accs[0]
        for c in range(1, n_lane):
            tot = tot + accs[c]
        mu = tot.sum(axis=-1, keepdims=True) * (1.0 / D_CONV)
        cen = [a - mu for a in accs]
        sq = cen[0] * cen[0]
        for c in range(1, n_lane):
            sq = sq + cen[c] * cen[c]
        inv = lax.rsqrt(sq.sum(axis=-1, keepdims=True) * (1.0 / D_CONV) + EPS)
        for c, cs in enumerate(lane_slices):
            y = cen[c] * inv * lg_ref[:, cs] + lb_ref[:, cs]
            o_ref[pl.ds(r0, CONV_ROWS), cs] = (y * _sigmoid(y)).astype(BF16)

    def body(rc, carry):
        for sub in range(CONV_CHUNKS_PER_ITER):
            row_chunk(pl.multiple_of((rc * CONV_CHUNKS_PER_ITER + sub) * CONV_ROWS, CONV_ROWS))
        return carry

    lax.fori_loop(0, TM // (CONV_ROWS * CONV_CHUNKS_PER_ITER), body, 0)


def _conv(ug, conv_w, conv_b, ln_g, ln_b):
    const = lambda i: (0, 0)
    halo_per_tile = TM // HALO
    return pl.pallas_call(
        _conv_kernel,
        grid=(TOKENS // TM,),
        in_specs=[
            pl.BlockSpec((TM, D_CONV), lambda i: (i, 0)),
            pl.BlockSpec((HALO, D_CONV), lambda i: (jnp.maximum(i * halo_per_tile - 1, 0), 0)),
            pl.BlockSpec((CONV_KERNEL, D_CONV), const),
            pl.BlockSpec((1, D_CONV), const),
            pl.BlockSpec((1, D_CONV), const),
            pl.BlockSpec((1, D_CONV), const),
        ],
        out_specs=pl.BlockSpec((TM, D_CONV), lambda i: (i, 0)),
        out_shape=jax.ShapeDtypeStruct((TOKENS, D_CONV), BF16),
        scratch_shapes=[pltpu.VMEM((SUBLANES, HALO + TM, D_CONV), F32)],
        compiler_params=pltpu.CompilerParams(
            dimension_semantics=("arbitrary",), vmem_limit_bytes=VMEM_LIMIT),
        name="conv_mixer",
    )(ug, ug, conv_w, conv_b, ln_g, ln_b)


def _attn_kernel(q_ref, kp_ref, kc_ref, vp_ref, vc_ref, bias_ref, o_ref, kw_ref, vt_ref):
    i = pl.program_id(0)
    seq_start = (i % TILES_PER_SEQ) == 0
    kw_ref[0:TM, :] = kp_ref[...]
    kw_ref[TM:, :] = kc_ref[...]
    vt_ref[:, 0:TM] = vp_ref[...].T
    vt_ref[:, TM:] = vc_ref[...].T
    pair = 2 * HEAD_DIM
    low = lax.broadcasted_iota(jnp.int32, (Q_PAIR, pair), 1) < HEAD_DIM

    def pair_block(cp, hp, k_lo):
        r0 = cp * Q_PAIR
        ps = slice(hp * pair, (hp + 1) * pair)
        q2 = q_ref[r0:r0 + Q_PAIR, ps]
        zero = jnp.zeros_like(q2)
        qm = jnp.concatenate([jnp.where(low, q2, zero), jnp.where(low, zero, q2)], axis=0)
        k2 = kw_ref[r0 + k_lo:r0 + KEY_WIN, ps]
        st = lax.dot_general(k2, qm, (((1,), (1,)), ((), ())), preferred_element_type=F32)
        st = st + bias_ref[hp, k_lo:, :]
        m = jnp.max(st, axis=0, keepdims=True)
        e = jnp.exp(st - m)
        l = jnp.sum(e, axis=0, keepdims=True)
        ot = jnp.dot(vt_ref[ps, r0 + k_lo:r0 + KEY_WIN], e.astype(BF16),
                     preferred_element_type=F32) * (1.0 / l)
        o2t = jnp.concatenate([ot[:HEAD_DIM, :Q_PAIR], ot[HEAD_DIM:, Q_PAIR:]], axis=0)
        o_ref[r0:r0 + Q_PAIR, ps] = o2t.T.astype(BF16)

    @pl.when(seq_start)
    def _():
        for cp in range(PAIRS_PER_TILE):
            for hp in range(N_HEADS // 2):
                pair_block(cp, hp, TM - cp * Q_PAIR)

    @pl.when(jnp.logical_not(seq_start))
    def _():
        for cp in range(PAIRS_PER_TILE):
            for hp in range(N_HEADS // 2):
                pair_block(cp, hp, 0)


def _attention(q, k, v, bias2):
    cur = lambda i: (i, 0)
    prev = lambda i: (jnp.maximum(i - 1, 0), 0)
    blk = (TM, D_ATTN)
    return pl.pallas_call(
        _attn_kernel,
        grid=(TOKENS // TM,),
        in_specs=[
            pl.BlockSpec(blk, cur),
            pl.BlockSpec(blk, prev), pl.BlockSpec(blk, cur),
            pl.BlockSpec(blk, prev), pl.BlockSpec(blk, cur),
            pl.BlockSpec((N_HEADS // 2, KEY_WIN, 2 * Q_PAIR), lambda i: (0, 0, 0)),
        ],
        out_specs=pl.BlockSpec(blk, cur),
        out_shape=jax.ShapeDtypeStruct((TOKENS, D_ATTN), BF16),
        scratch_shapes=[pltpu.VMEM((2 * TM, D_ATTN), BF16), pltpu.VMEM((D_ATTN, 2 * TM), BF16)],
        compiler_params=pltpu.CompilerParams(
            dimension_semantics=("arbitrary",), vmem_limit_bytes=VMEM_LIMIT),
        name="chunk_attention",
    )(q, k, k, v, v, bias2)


def _store_row_tiles(ref, base, rows, value):
    for c in range(ROW_TILE):
        ref[pl.ds(base + c, rows, stride=ROW_TILE), :] = value[:, c * LANES:(c + 1) * LANES]


def _load_row_tile_chunk(ref, base, rows, c):
    return ref[pl.ds(base + c, rows, stride=ROW_TILE), :]


def _tile_copy(src_ref, src_row, dst_ref, dst_row, sem):
    return pltpu.make_async_copy(
        src_ref.at[pl.ds(src_row * ROW_TILE, ROW_TILE), :],
        dst_ref.at[pl.ds(dst_row * ROW_TILE, ROW_TILE), :], sem)


def _outproj_kernel(yc_ref, ya_ref, x_ref, g1_ref, wt_ref, wb_ref, sh_ref, sc_ref, g_ref,
                    rwh_ref, rwl_ref, rb_ref, tri_ref,
                    x1_ref, dest_ref, gate_ref, cnt_ref, xs_hbm,
                    hbuf, dvm, dsm, cvm, csm, zbuf, carry, sem_idx, sem_push, sem_zero):
    i = pl.program_id(0)
    n = pl.num_programs(0)
    slot = i % 2

    @pl.when(i == 0)
    def _():
        carry[...] = jnp.zeros_like(carry)

    mixed = (jnp.dot(yc_ref[...], wt_ref[...], preferred_element_type=F32)
             + jnp.dot(ya_ref[...], wb_ref[...], preferred_element_type=F32))
    x1 = x_ref[...] + g1_ref[0] * mixed
    x1_ref[...] = x1
    h2 = _modulated_norm(x1, g_ref[...], sc_ref[0], sh_ref[0])

    hi = h2.astype(BF16)
    lo = (h2 - hi.astype(F32)).astype(BF16)
    logits = (jnp.dot(hi, rwh_ref[...], preferred_element_type=F32)
              + jnp.dot(hi, rwl_ref[...], preferred_element_type=F32)
              + jnp.dot(lo, rwh_ref[...], preferred_element_type=F32)) + rb_ref[...]
    lt = logits.T[:N_EXPERTS, :]

    eid = lax.broadcasted_iota(I32, (N_EXPERTS, TM), 0)
    picks, onehots, exps = [], [], []
    top0 = None
    for kk in range(TOP_K):
        m = jnp.max(lt, axis=0, keepdims=True)
        am = jnp.min(jnp.where(lt == m, eid, N_EXPERTS), axis=0, keepdims=True)
        oh = eid == am
        lt = jnp.where(oh, NEG_INF, lt)
        if kk == 0:
            top0 = m
        picks.append(am)
        onehots.append(oh)
        exps.append(jnp.exp(m - top0))
    denom = exps[0] + exps[1] + exps[2] + exps[3]
    gate_ref[...] = jnp.concatenate(exps, axis=0) * (1.0 / denom)

    oh_all = (onehots[0].astype(F32) + onehots[1].astype(F32)
              + onehots[2].astype(F32) + onehots[3].astype(F32))
    before = jnp.dot(oh_all.astype(BF16), tri_ref[...], preferred_element_type=F32) + carry[:, 0:1]
    dests = []
    for kk in range(TOP_K):
        rank = jnp.sum(jnp.where(onehots[kk], before, 0.0), axis=0, keepdims=True)
        dests.append(picks[kk] * EXPERT_CAP + rank.astype(I32))
    dest = jnp.concatenate(dests, axis=0)
    dest_ref[...] = dest
    carry[...] = carry[...] + jnp.sum(oh_all, axis=1, keepdims=True)
    cnt_ref[...] = carry[...]

    dvm[...] = jnp.concatenate([dest, jnp.zeros((SUBLANES - TOP_K, TM), I32)], axis=0)
    idx_copy = pltpu.make_async_copy(dvm, dsm, sem_idx)
    idx_copy.start()

    hb = pl.multiple_of(slot * (TM * ROW_TILE), TM * ROW_TILE)
    _store_row_tiles(hbuf, hb, TM, h2)

    def wait_push(s):
        base = pl.multiple_of(s * (TM * ROW_TILE), TM * ROW_TILE)
        for _ in range(TOP_K):
            pltpu.make_async_copy(hbuf.at[pl.ds(base, TM * ROW_TILE), :],
                                  xs_hbm.at[pl.ds(0, TM * ROW_TILE), :], sem_push.at[s]).wait()

    @pl.when(i > 0)
    def _():
        wait_push(1 - slot)

    idx_copy.wait()
    per_iter = DMA_UNROLL // TOP_K

    def push_body(r, c):
        for u in range(DMA_UNROLL):
            t = r * per_iter + u // TOP_K
            d = dsm[u % TOP_K, t]
            _tile_copy(hbuf, slot * TM + t, xs_hbm, d, sem_push.at[slot]).start(priority=u % 2)
        return c

    lax.fori_loop(0, TM // per_iter, push_body, 0)

    @pl.when(i == n - 1)
    def _():
        wait_push(slot)
        zbuf[...] = jnp.zeros_like(zbuf)
        cvm[...] = carry[...].astype(I32)
        cnt_copy = pltpu.make_async_copy(cvm, csm, sem_idx)
        cnt_copy.start()
        cnt_copy.wait()

        def per_expert(e, c, wait):
            cnt = csm[e, 0]
            end = (cnt + MOE_M - 1) // MOE_M * MOE_M

            def row(r, c2):
                cp = _tile_copy(zbuf, 0, xs_hbm, e * EXPERT_CAP + r, sem_zero)
                if wait:
                    cp.wait()
                else:
                    cp.start()
                return c2

            lax.fori_loop(cnt, end, row, 0)
            return c

        lax.fori_loop(0, N_EXPERTS, lambda e, c: per_expert(e, c, False), 0)
        lax.fori_loop(0, N_EXPERTS, lambda e, c: per_expert(e, c, True), 0)


def _outproj(yc, ya, x, mod3, w_top, w_bot, norm_g, rw_hi, rw_lo, rb, tri):
    row = lambda i: (i, 0)
    col = lambda i: (0, i)
    const = lambda i: (0, 0)
    modspec = lambda j: pl.BlockSpec((1, 1, D_MODEL), lambda i: (i // TILES_PER_SEQ, 0, j))
    return pl.pallas_call(
        _outproj_kernel,
        grid=(TOKENS // TM,),
        in_specs=[
            pl.BlockSpec((TM, D_CONV), row),
            pl.BlockSpec((TM, D_ATTN), row),
            pl.BlockSpec((TM, D_MODEL), row),
            modspec(2),
            pl.BlockSpec((D_CONV, D_MODEL), const),
            pl.BlockSpec((D_ATTN, D_MODEL), const),
            modspec(3),
            modspec(4),
            pl.BlockSpec((1, D_MODEL), const),
            pl.BlockSpec((D_MODEL, LANES), const),
            pl.BlockSpec((D_MODEL, LANES), const),
            pl.BlockSpec((1, LANES), const),
            pl.BlockSpec((TM, TM), const),
        ],
        out_specs=[
            pl.BlockSpec((TM, D_MODEL), row),
            pl.BlockSpec((TOP_K, TM), col),
            pl.BlockSpec((TOP_K, TM), col),
            pl.BlockSpec((N_EXPERTS, LANES), const),
            pl.BlockSpec(memory_space=pl.ANY),
        ],
        out_shape=[
            jax.ShapeDtypeStruct((TOKENS, D_MODEL), F32),
            jax.ShapeDtypeStruct((TOP_K, TOKENS), I32),
            jax.ShapeDtypeStruct((TOP_K, TOKENS), F32),
            jax.ShapeDtypeStruct((N_EXPERTS, LANES), F32),
            jax.ShapeDtypeStruct((N_EXPERTS * EXPERT_CAP * ROW_TILE, LANES), F32),
        ],
        scratch_shapes=[
            pltpu.VMEM((2 * TM * ROW_TILE, LANES), F32),
            pltpu.VMEM((SUBLANES, TM), I32),
            pltpu.SMEM((SUBLANES, TM), I32),
            pltpu.VMEM((N_EXPERTS, LANES), I32),
            pltpu.SMEM((N_EXPERTS, LANES), I32),
            pltpu.VMEM((ROW_TILE, LANES), F32),
            pltpu.VMEM((N_EXPERTS, LANES), F32),
            pltpu.SemaphoreType.DMA(()),
            pltpu.SemaphoreType.DMA((2,)),
            pltpu.SemaphoreType.DMA(()),
        ],
        compiler_params=pltpu.CompilerParams(
            dimension_semantics=("arbitrary",), vmem_limit_bytes=VMEM_LIMIT),
        name="outproj_router",
    )(yc, ya, x, mod3, w_top, w_bot, mod3, mod3, norm_g, rw_hi, rw_lo, rb, tri)


def _expert_kernel(nblk_ref, xs_hbm, w1_ref, b1_ref, w2_ref, b2_ref, ys_hbm,
                   xin, yout, w1b, w2b, sem_in, sem_out):
    e = pl.program_id(0)
    nb = nblk_ref[e]
    blk = MOE_M * ROW_TILE
    first = e * (EXPERT_CAP * ROW_TILE)

    def in_copy(j, s):
        return pltpu.make_async_copy(
            xs_hbm.at[pl.ds(pl.multiple_of(first + j * blk, blk), blk), :],
            xin.at[pl.ds(pl.multiple_of(s * blk, blk), blk), :], sem_in.at[s])

    def out_copy(j, s):
        return pltpu.make_async_copy(
            yout.at[pl.ds(pl.multiple_of(s * blk, blk), blk), :],
            ys_hbm.at[pl.ds(pl.multiple_of(first + j * blk, blk), blk), :], sem_out.at[s])

    @pl.when(nb > 0)
    def _():
        in_copy(0, 0).start(priority=1)
        w1b[...] = w1_ref[0, 0].astype(BF16)
        w2b[...] = w2_ref[0, 0].astype(BF16)

        def body(j, carry):
            s = j % 2
            base = pl.multiple_of(s * blk, blk)

            @pl.when(j + 1 < nb)
            def _():
                in_copy(j + 1, 1 - s).start(priority=1)

            in_copy(j, s).wait()
            xs = jnp.concatenate(
                [_load_row_tile_chunk(xin, base, MOE_M, c) for c in range(ROW_TILE)], axis=1).astype(BF16)
            u = jnp.dot(xs, w1b[...], preferred_element_type=F32) + b1_ref[0, 0]
            glu = jnp.minimum(u[:, :D_EXPERT], SWIGLU_LIMIT)
            lin = jnp.clip(u[:, D_EXPERT:], -SWIGLU_LIMIT, SWIGLU_LIMIT)
            act = glu * _sigmoid(SWIGLU_ALPHA * glu) * (lin + 1.0)
            y = jnp.dot(act.astype(BF16), w2b[...], preferred_element_type=F32) + b2_ref[0, 0]

            @pl.when(j >= 2)
            def _():
                out_copy(j - 2, s).wait()

            _store_row_tiles(yout, base, MOE_M, y)
            out_copy(j, s).start(priority=1)
            return carry

        lax.fori_loop(0, nb, body, 0)

        @pl.when(nb >= 2)
        def _():
            out_copy(nb - 2, nb % 2).wait()

        out_copy(nb - 1, (nb - 1) % 2).wait()


def _experts(layer, nblk, xs, w1, b1, w2, b2):
    wmap = lambda e, nb: (layer, e, 0, 0)
    blk = MOE_M * ROW_TILE
    grid_spec = pltpu.PrefetchScalarGridSpec(
        num_scalar_prefetch=1,
        grid=(N_EXPERTS,),
        in_specs=[
            pl.BlockSpec(memory_space=pl.ANY),
            pl.BlockSpec((1, 1, D_MODEL, 2 * D_EXPERT), wmap),
            pl.BlockSpec((1, 1, 1, 2 * D_EXPERT), wmap),
            pl.BlockSpec((1, 1, D_EXPERT, D_MODEL), wmap),
            pl.BlockSpec((1, 1, 1, D_MODEL), wmap),
        ],
        out_specs=pl.BlockSpec(memory_space=pl.ANY),
        scratch_shapes=[
            pltpu.VMEM((2 * blk, LANES), F32),
            pltpu.VMEM((2 * blk, LANES), F32),
            pltpu.VMEM((D_MODEL, 2 * D_EXPERT), BF16),
            pltpu.VMEM((D_EXPERT, D_MODEL), BF16),
            pltpu.SemaphoreType.DMA((2,)),
            pltpu.SemaphoreType.DMA((2,)),
        ],
    )
    return pl.pallas_call(
        _expert_kernel,
        grid_spec=grid_spec,
        out_shape=jax.ShapeDtypeStruct((N_EXPERTS * EXPERT_CAP * ROW_TILE, LANES), F32),
        compiler_params=pltpu.CompilerParams(
            dimension_semantics=("arbitrary",), vmem_limit_bytes=VMEM_LIMIT),
        name="experts",
    )(nblk, xs, w1, b1.reshape(DEPTH, N_EXPERTS, 1, 2 * D_EXPERT), w2,
      b2.reshape(DEPTH, N_EXPERTS, 1, D_MODEL))


def _gather_combine(tile, idx0_ref, idxn_ref, ys_hbm, x1_ref, g2_ref, gate_ref, o_ref, bufs, sem,
                    then=None):
    i = pl.program_id(0)
    n = pl.num_programs(0)
    rows = tile * TOP_K

    def wait_gather(s):
        pltpu.make_async_copy(ys_hbm.at[pl.ds(0, rows * ROW_TILE), :], bufs[s], sem.at[s]).wait()

    @pl.when(i == 0)
    def _():
        def body(r, c):
            for u in range(DMA_UNROLL):
                j = r * DMA_UNROLL + u
                _tile_copy(ys_hbm, idx0_ref[0, 0, j], bufs[0], j, sem.at[0]).start(priority=u % 2)
            return c

        lax.fori_loop(0, rows // DMA_UNROLL, body, 0)

    def step(cur, nxt):
        wait_gather(cur)
        gates = gate_ref[...]
        g2 = g2_ref[0]
        for c in range(ROW_TILE):
            cs = slice(c * LANES, (c + 1) * LANES)
            acc = jnp.zeros((tile, LANES), F32)
            for kk in range(TOP_K):
                acc = acc + gates[:, kk:kk + 1] * _load_row_tile_chunk(
                    bufs[cur], kk * tile * ROW_TILE, tile, c)
            o_ref[:, cs] = x1_ref[:, cs] + g2[:, cs] * acc
        if then is not None:
            then()
        for j in range(rows):
            _tile_copy(ys_hbm, idxn_ref[0, 0, j], bufs[nxt], j, sem.at[nxt]).start(priority=j % 2)

        @pl.when(i == n - 1)
        def _():
            wait_gather(nxt)

    @pl.when(i % 2 == 0)
    def _():
        step(0, 1)

    @pl.when(i % 2 == 1)
    def _():
        step(1, 0)


def _combine_kernel(idx0_ref, idxn_ref, ys_hbm, x1_ref, g2_ref, gate_ref, o_ref, buf0, buf1, sem):
    _gather_combine(TMC, idx0_ref, idxn_ref, ys_hbm, x1_ref, g2_ref, gate_ref, o_ref, (buf0, buf1), sem)


def _combine_inproj_kernel(idx0_ref, idxn_ref, ys_hbm, x1_ref, g2_ref, gate_ref,
                           sh_ref, sc_ref, g_ref, w_ref, seg_ref, qg_ref, kg_ref,
                           xn_ref, ug_ref, q_ref, k_ref, v_ref, buf0, buf1, sem):
    def inproj():
        _inproj_body(xn_ref[...], sh_ref, sc_ref, g_ref, w_ref, seg_ref, qg_ref, kg_ref,
                     ug_ref, q_ref, k_ref, v_ref)

    _gather_combine(TM, idx0_ref, idxn_ref, ys_hbm, x1_ref, g2_ref, gate_ref, xn_ref, (buf0, buf1), sem,
                    then=inproj)


def _choice_major(dest, tile):
    return dest.reshape(TOP_K, TOKENS // tile, tile).transpose(1, 0, 2).reshape(
        TOKENS // tile, 1, TOP_K * tile)


def _combine_inproj(dest, ys, x1, mod3_prev, gates, mod3, norm_g, w_in_bf, seg, qg, kg):
    n_tiles = TOKENS // TM
    rows = TM * TOP_K
    row = lambda i: (i, 0)
    const = lambda i: (0, 0)
    modspec = lambda j: pl.BlockSpec((1, 1, D_MODEL), lambda i: (i // TILES_PER_SEQ, 0, j))
    half_sd = jax.ShapeDtypeStruct((TOKENS, D_CONV), BF16)
    return pl.pallas_call(
        _combine_inproj_kernel,
        grid=(n_tiles,),
        in_specs=[
            pl.BlockSpec((1, 1, rows), lambda i: (0, 0, 0), memory_space=pltpu.SMEM),
            pl.BlockSpec((1, 1, rows), lambda i: (jnp.minimum(i + 1, n_tiles - 1), 0, 0),
                         memory_space=pltpu.SMEM),
            pl.BlockSpec(memory_space=pl.ANY),
            pl.BlockSpec((TM, D_MODEL), row),
            modspec(5),
            pl.BlockSpec((TM, TOP_K), row),
            modspec(0),
            modspec(1),
            pl.BlockSpec((1, D_MODEL), const),
            pl.BlockSpec((D_MODEL, D_IN), const),
            pl.BlockSpec((D_ATTN, D_ATTN), const),
            pl.BlockSpec((1, D_ATTN), const),
            pl.BlockSpec((1, D_ATTN), const),
        ],
        out_specs=[pl.BlockSpec((TM, D_MODEL), row)] + [pl.BlockSpec((TM, D_CONV), row)] * 4,
        out_shape=[jax.ShapeDtypeStruct((TOKENS, D_MODEL), F32)] + [half_sd] * 4,
        scratch_shapes=[
            pltpu.VMEM((rows * ROW_TILE, LANES), F32),
            pltpu.VMEM((rows * ROW_TILE, LANES), F32),
            pltpu.SemaphoreType.DMA((2,)),
        ],
        compiler_params=pltpu.CompilerParams(
            dimension_semantics=("arbitrary",), vmem_limit_bytes=VMEM_LIMIT),
        name="combine_inproj",
    )(_choice_major(dest, TM), _choice_major(dest, TM), ys, x1, mod3_prev, gates,
      mod3, mod3, norm_g, w_in_bf, seg, qg, kg)


def _combine(dest3, ys, x1, mod3, gates):
    n_tiles = TOKENS // TMC
    tiles_per_seq = SEQ // TMC
    rows = TMC * TOP_K
    return pl.pallas_call(
        _combine_kernel,
        grid=(n_tiles,),
        in_specs=[
            pl.BlockSpec((1, 1, rows), lambda i: (0, 0, 0), memory_space=pltpu.SMEM),
            pl.BlockSpec((1, 1, rows), lambda i: (jnp.minimum(i + 1, n_tiles - 1), 0, 0),
                         memory_space=pltpu.SMEM),
            pl.BlockSpec(memory_space=pl.ANY),
            pl.BlockSpec((TMC, D_MODEL), lambda i: (i, 0)),
            pl.BlockSpec((1, 1, D_MODEL), lambda i: (i // tiles_per_seq, 0, 5)),
            pl.BlockSpec((TMC, TOP_K), lambda i: (i, 0)),
        ],
        out_specs=pl.BlockSpec((TMC, D_MODEL), lambda i: (i, 0)),
        out_shape=jax.ShapeDtypeStruct((TOKENS, D_MODEL), F32),
        scratch_shapes=[
            pltpu.VMEM((rows * ROW_TILE, LANES), F32),
            pltpu.VMEM((rows * ROW_TILE, LANES), F32),
            pltpu.SemaphoreType.DMA((2,)),
        ],
        compiler_params=pltpu.CompilerParams(
            dimension_semantics=("arbitrary",), vmem_limit_bytes=VMEM_LIMIT),
        name="moe_combine",
    )(dest3, dest3, ys, x1, mod3, gates)


_I = np.arange(Q_PAIR)[:, None]
_K = np.arange(KEY_WIN)[None, :]
_BAND_OK = (_K // CHUNK >= _I // CHUNK) & (_K // CHUNK <= _I // CHUNK + LEFT_CHUNKS)
_SEG = np.kron(np.eye(N_HEADS), np.full((HEAD_DIM, HEAD_DIM), 1.0 / HEAD_DIM)).astype(np.float32)
_TRI = (np.arange(TM)[:, None] < np.arange(TM)[None, :]).astype(np.float32)


def _bias_tables(rel_bias):
    n_far = KEY_WIN - 1 + Q_PAIR - 2 * REL_CLIP
    ext = jnp.concatenate(
        [rel_bias[..., 1:], jnp.repeat(rel_bias[..., 2 * REL_CLIP:], n_far, axis=-1)], axis=-1)
    rev = ext[..., ::-1]
    period = KEY_WIN + Q_PAIR
    ring = jnp.concatenate(
        [rev[..., Q_PAIR - 1:], jnp.zeros(rev.shape[:-1] + (1,), rev.dtype), rev[..., :Q_PAIR - 1]],
        axis=-1)
    lead = rel_bias.shape[:-1]
    skew = jnp.tile(ring, Q_PAIR)[..., :Q_PAIR * (period - 1)].reshape(lead + (Q_PAIR, period - 1))
    table = jnp.where(_BAND_OK, skew[..., :KEY_WIN], NEG_INF).astype(F32)
    return table.reshape(DEPTH, N_HEADS // 2, 2, Q_PAIR, KEY_WIN).transpose(0, 1, 4, 2, 3).reshape(
        DEPTH, N_HEADS // 2, KEY_WIN, 2 * Q_PAIR)


def kernel(x, c, ada_w, ada_b, norm1_g, w_in, conv_w, conv_b, conv_ln_g, conv_ln_b, q_norm_g,
           k_norm_g, rel_bias, w_out, norm2_g, router_w, router_b, exp_w1, exp_b1, exp_w2, exp_b2):
    xf = x.reshape(TOKENS, D_MODEL)
    mod = _ada_mod(c, ada_w, ada_b)
    seg = jnp.asarray(_SEG, BF16)
    tri = jnp.asarray(_TRI, BF16)
    pad_e = ((0, 0), (0, LANES - N_EXPERTS))
    bias_tables = _bias_tables(rel_bias)
    moe = None
    for l in range(DEPTH):
        mod3 = mod[l].reshape(BATCH, 1, 6 * D_MODEL)
        qg = (jnp.tile(q_norm_g[l], N_HEADS) * (HEAD_DIM ** -0.5)).reshape(1, D_ATTN)
        kg = jnp.tile(k_norm_g[l], N_HEADS).reshape(1, D_ATTN)
        inproj_params = (mod3, norm1_g[l].reshape(1, D_MODEL), w_in[l].astype(BF16), seg, qg, kg)
        if moe is None:
            ug, q, k, v = _inproj(xf, *inproj_params)
        else:
            xf, ug, q, k, v = _combine_inproj(*moe, *inproj_params)
        y_conv = _conv(ug, conv_w[l], conv_b[l].reshape(1, D_CONV),
                       conv_ln_g[l].reshape(1, D_CONV), conv_ln_b[l].reshape(1, D_CONV))
        y_attn = _attention(q, k, v, bias_tables[l])
        w_out_bf = w_out[l].astype(BF16)
        rw_hi = router_w[l].astype(BF16)
        rw_lo = (router_w[l] - rw_hi.astype(F32)).astype(BF16)
        x1, dest, gates, cnt, xs = _outproj(
            y_conv, y_attn, xf, mod3, w_out_bf[:D_CONV], w_out_bf[D_CONV:],
            norm2_g[l].reshape(1, D_MODEL), jnp.pad(rw_hi, pad_e), jnp.pad(rw_lo, pad_e),
            jnp.pad(router_b[l].reshape(1, N_EXPERTS), pad_e), tri)
        nblk = (cnt[:, 0].astype(I32) + MOE_M - 1) // MOE_M
        ys = _experts(l, nblk, xs, exp_w1, exp_b1, exp_w2, exp_b2)
        moe = (dest, ys, x1, mod3, gates.T)
    dest, ys, x1, mod3, gates_t = moe
    xf = _combine(_choice_major(dest, TMC), ys, x1, mod3, gates_t)
    return xf.reshape(BATCH, SEQ, D_MODEL)
```

```python
import jax
import jax.numpy as jnp
import numpy as np
from jax import lax
from jax.experimental import pallas as pl
from jax.experimental.pallas import tpu as pltpu

D_MODEL = 1024
BATCH = 8
SEQ = 2048
DEPTH = 4
TOKENS = BATCH * SEQ

CHUNK = 64
D_CONV = D_MODEL // 2
D_ATTN = D_MODEL - D_CONV
N_HEADS = 8
HEAD_DIM = D_ATTN // N_HEADS
CONV_KERNEL = 31
LEFT_CHUNKS = 8
REL_CLIP = 128
N_EXPERTS = 32
TOP_K = 4
D_EXPERT = D_MODEL
SWIGLU_ALPHA = 1.702
SWIGLU_LIMIT = 7.0
EPS = 1e-6
D_IN = 2 * D_CONV + 3 * D_ATTN

F32 = jnp.float32
BF16 = jnp.bfloat16
I32 = jnp.int32
NEG_INF = float("-inf")

LANES = 128
SUBLANES = 8
TM = 512
TILES_PER_SEQ = SEQ // TM
Q_PAIR = 2 * CHUNK
KEY_WIN = (LEFT_CHUNKS + 2) * CHUNK
PAIRS_PER_TILE = TM // Q_PAIR
HALO = 32
CONV_ROWS = 32
CONV_CHUNKS_PER_ITER = 4
MOE_M = 512
ZERO_ROWS = MOE_M // 2
EXPERT_CAP = TOKENS
ROW_TILE = D_MODEL // LANES
TMC = 256
DMA_UNROLL = 8

VMEM_LIMIT = 56 * 1024 * 1024


def _sigmoid(x):
    return 1.0 / (1.0 + jnp.exp(-x))


def _ada_kernel(c_ref, w_ref, b_ref, o_ref):
    c = c_ref[...]
    ca = (c * _sigmoid(c)).astype(BF16)
    o_ref[0] = jnp.dot(ca, w_ref[0].astype(BF16), preferred_element_type=F32) + b_ref[0]


def _ada_mod(c, ada_w, ada_b):
    n_tile = 1536
    return pl.pallas_call(
        _ada_kernel,
        grid=(DEPTH, 6 * D_MODEL // n_tile),
        in_specs=[
            pl.BlockSpec((BATCH, D_MODEL), lambda l, n: (0, 0)),
            pl.BlockSpec((1, D_MODEL, n_tile), lambda l, n: (l, 0, n)),
            pl.BlockSpec((1, 1, n_tile), lambda l, n: (l, 0, n)),
        ],
        out_specs=pl.BlockSpec((1, BATCH, n_tile), lambda l, n: (l, 0, n)),
        out_shape=jax.ShapeDtypeStruct((DEPTH, BATCH, 6 * D_MODEL), F32),
        compiler_params=pltpu.CompilerParams(
            dimension_semantics=("arbitrary", "arbitrary"), vmem_limit_bytes=VMEM_LIMIT),
        name="ada_mod",
    )(c, ada_w, ada_b.reshape(DEPTH, 1, 6 * D_MODEL))


def _modulated_norm(x, g, sc, sh):
    ms = jnp.mean(x * x, axis=-1, keepdims=True)
    return x * lax.rsqrt(ms + EPS) * g * (1.0 + sc) + sh


def _inproj_kernel(x_ref, *refs):
    _inproj_body(x_ref[...], *refs)


def _inproj_body(x, sh_ref, sc_ref, g_ref, w_ref, seg_ref, qg_ref, kg_ref, ug_ref, q_ref, k_ref, v_ref):
    h = _modulated_norm(x, g_ref[...], sc_ref[0], sh_ref[0])
    u = jnp.dot(h.astype(BF16), w_ref[...], preferred_element_type=F32)
    a = u[:, :D_CONV]
    gt = u[:, D_CONV:2 * D_CONV]
    ug_ref[...] = (a * _sigmoid(gt)).astype(BF16)
    o = 2 * D_CONV
    q = u[:, o:o + D_ATTN]
    k = u[:, o + D_ATTN:o + 2 * D_ATTN]
    v = u[:, o + 2 * D_ATTN:]
    qms = jnp.dot((q * q).astype(BF16), seg_ref[...], preferred_element_type=F32)
    kms = jnp.dot((k * k).astype(BF16), seg_ref[...], preferred_element_type=F32)
    q_ref[...] = (q * lax.rsqrt(qms + EPS) * qg_ref[...]).astype(BF16)
    k_ref[...] = (k * lax.rsqrt(kms + EPS) * kg_ref[...]).astype(BF16)
    v_ref[...] = v.astype(BF16)


def _inproj(x, mod3, norm_g, w_in_bf, seg, qg, kg):
    row = lambda i: (i, 0)
    const = lambda i: (0, 0)
    out_sd = jax.ShapeDtypeStruct((TOKENS, D_CONV), BF16)
    return pl.pallas_call(
        _inproj_kernel,
        grid=(TOKENS // TM,),
        in_specs=[
            pl.BlockSpec((TM, D_MODEL), row),
            pl.BlockSpec((1, 1, D_MODEL), lambda i: (i // TILES_PER_SEQ, 0, 0)),
            pl.BlockSpec((1, 1, D_MODEL), lambda i: (i // TILES_PER_SEQ, 0, 1)),
            pl.BlockSpec((1, D_MODEL), const),
            pl.BlockSpec((D_MODEL, D_IN), const),
            pl.BlockSpec((D_ATTN, D_ATTN), const),
            pl.BlockSpec((1, D_ATTN), const),
            pl.BlockSpec((1, D_ATTN), const),
        ],
        out_specs=[pl.BlockSpec((TM, D_CONV), row)] * 4,
        out_shape=[out_sd] * 4,
        compiler_params=pltpu.CompilerParams(
            dimension_semantics=("arbitrary",), vmem_limit_bytes=VMEM_LIMIT),
        name="inproj",
    )(x, mod3, mod3, norm_g, w_in_bf, seg, qg, kg)


def _conv_kernel(cur_ref, halo_ref, w_ref, b_ref, lg_ref, lb_ref, o_ref, xe_ref):
    i = pl.program_id(0)
    seq_start = (i % TILES_PER_SEQ) == 0
    halo = halo_ref[...].astype(F32)
    xe_ref[0, 0:HALO, :] = jnp.where(seq_start, 0.0, halo)
    xe_ref[0, HALO:, :] = cur_ref[...].astype(F32)
    ext = HALO + TM
    for s in range(1, SUBLANES):
        xe_ref[s, 0:ext - SUBLANES, :] = xe_ref[0, s:ext - SUBLANES + s, :]
    n_lane = D_CONV // LANES
    first_tap = HALO - (CONV_KERNEL - 1)

    lane_slices = [slice(c * LANES, (c + 1) * LANES) for c in range(n_lane)]

    def row_chunk(r0):
        accs = [jnp.zeros((CONV_ROWS, LANES), F32) + b_ref[:, cs] for cs in lane_slices]
        for j in range(CONV_KERNEL):
            shift = (first_tap + j) % SUBLANES
            aligned = first_tap + j - shift
            for c, cs in enumerate(lane_slices):
                accs[c] = accs[c] + w_ref[j:j + 1, cs] * xe_ref[shift, pl.ds(r0 + aligned, CONV_ROWS), cs]
        tot = accs[0]
        for c in range(1, n_lane):
            tot = tot + accs[c]
        mu = tot.sum(axis=-1, keepdims=True) * (1.0 / D_CONV)
        cen = [a - mu for a in accs]
        sq = cen[0] * cen[0]
        for c in range(1, n_lane):
            sq = sq + cen[c] * cen[c]
        inv = lax.rsqrt(sq.sum(axis=-1, keepdims=True) * (1.0 / D_CONV) + EPS)
        for c, cs in enumerate(lane_slices):
            y = cen[c] * inv * lg_ref[:, cs] + lb_ref[:, cs]
            o_ref[pl.ds(r0, CONV_ROWS), cs] = (y * _sigmoid(y)).astype(BF16)

    def body(rc, carry):
        for sub in range(CONV_CHUNKS_PER_ITER):
            row_chunk(pl.multiple_of((rc * CONV_CHUNKS_PER_ITER + sub) * CONV_ROWS, CONV_ROWS))
        return carry

    lax.fori_loop(0, TM // (CONV_ROWS * CONV_CHUNKS_PER_ITER), body, 0)


def _conv(ug, conv_w, conv_b, ln_g, ln_b):
    const = lambda i: (0, 0)
    halo_per_tile = TM // HALO
    return pl.pallas_call(
        _conv_kernel,
        grid=(TOKENS // TM,),
        in_specs=[
            pl.BlockSpec((TM, D_CONV), lambda i: (i, 0)),
            pl.BlockSpec((HALO, D_CONV), lambda i: (jnp.maximum(i * halo_per_tile - 1, 0), 0)),
            pl.BlockSpec((CONV_KERNEL, D_CONV), const),
            pl.BlockSpec((1, D_CONV), const),
            pl.BlockSpec((1, D_CONV), const),
            pl.BlockSpec((1, D_CONV), const),
        ],
        out_specs=pl.BlockSpec((TM, D_CONV), lambda i: (i, 0)),
        out_shape=jax.ShapeDtypeStruct((TOKENS, D_CONV), BF16),
        scratch_shapes=[pltpu.VMEM((SUBLANES, HALO + TM, D_CONV), F32)],
        compiler_params=pltpu.CompilerParams(
            dimension_semantics=("arbitrary",), vmem_limit_bytes=VMEM_LIMIT),
        name="conv_mixer",
    )(ug, ug, conv_w, conv_b, ln_g, ln_b)


def _attn_kernel(q_ref, kp_ref, kc_ref, vp_ref, vc_ref, bias_ref, o_ref, kw_ref, vt_ref):
    i = pl.program_id(0)
    seq_start = (i % TILES_PER_SEQ) == 0
    kw_ref[0:TM, :] = kp_ref[...]
    kw_ref[TM:, :] = kc_ref[...]
    vt_ref[:, 0:TM] = vp_ref[...].T
    vt_ref[:, TM:] = vc_ref[...].T
    pair = 2 * HEAD_DIM
    low = lax.broadcasted_iota(jnp.int32, (Q_PAIR, pair), 1) < HEAD_DIM

    def pair_block(cp, hp, k_lo):
        r0 = cp * Q_PAIR
        ps = slice(hp * pair, (hp + 1) * pair)
        q2 = q_ref[r0:r0 + Q_PAIR, ps]
        zero = jnp.zeros_like(q2)
        qm = jnp.concatenate([jnp.where(low, q2, zero), jnp.where(low, zero, q2)], axis=0)
        k2 = kw_ref[r0 + k_lo:r0 + KEY_WIN, ps]
        st = lax.dot_general(k2, qm, (((1,), (1,)), ((), ())), preferred_element_type=F32)
        st = st + bias_ref[hp, k_lo:, :]
        m = jnp.max(st, axis=0, keepdims=True)
        e = jnp.exp(st - m)
        l = jnp.sum(e, axis=0, keepdims=True)
        ot = jnp.dot(vt_ref[ps, r0 + k_lo:r0 + KEY_WIN], e.astype(BF16),
                     preferred_element_type=F32) * (1.0 / l)
        o2t = jnp.concatenate([ot[:HEAD_DIM, :Q_PAIR], ot[HEAD_DIM:, Q_PAIR:]], axis=0)
        o_ref[r0:r0 + Q_PAIR, ps] = o2t.T.astype(BF16)

    @pl.when(seq_start)
    def _():
        for cp in range(PAIRS_PER_TILE):
            for hp in range(N_HEADS // 2):
                pair_block(cp, hp, TM - cp * Q_PAIR)

    @pl.when(jnp.logical_not(seq_start))
    def _():
        for cp in range(PAIRS_PER_TILE):
            for hp in range(N_HEADS // 2):
                pair_block(cp, hp, 0)


def _attention(q, k, v, bias2):
    cur = lambda i: (i, 0)
    prev = lambda i: (jnp.maximum(i - 1, 0), 0)
    blk = (TM, D_ATTN)
    return pl.pallas_call(
        _attn_kernel,
        grid=(TOKENS // TM,),
        in_specs=[
            pl.BlockSpec(blk, cur),
            pl.BlockSpec(blk, prev), pl.BlockSpec(blk, cur),
            pl.BlockSpec(blk, prev), pl.BlockSpec(blk, cur),
            pl.BlockSpec((N_HEADS // 2, KEY_WIN, 2 * Q_PAIR), lambda i: (0, 0, 0)),
        ],
        out_specs=pl.BlockSpec(blk, cur),
        out_shape=jax.ShapeDtypeStruct((TOKENS, D_ATTN), BF16),
        scratch_shapes=[pltpu.VMEM((2 * TM, D_ATTN), BF16), pltpu.VMEM((D_ATTN, 2 * TM), BF16)],
        compiler_params=pltpu.CompilerParams(
            dimension_semantics=("arbitrary",), vmem_limit_bytes=VMEM_LIMIT),
        name="chunk_attention",
    )(q, k, k, v, v, bias2)


def _store_row_tiles(ref, base, rows, value):
    for c in range(ROW_TILE):
        ref[pl.ds(base + c, rows, stride=ROW_TILE), :] = value[:, c * LANES:(c + 1) * LANES]


def _load_row_tile_chunk(ref, base, rows, c):
    return ref[pl.ds(base + c, rows, stride=ROW_TILE), :]


def _tile_copy(src_ref, src_row, dst_ref, dst_row, sem):
    return pltpu.make_async_copy(
        src_ref.at[pl.ds(src_row * ROW_TILE, ROW_TILE), :],
        dst_ref.at[pl.ds(dst_row * ROW_TILE, ROW_TILE), :], sem)


def _outproj_kernel(yc_ref, ya_ref, x_ref, g1_ref, wt_ref, wb_ref, sh_ref, sc_ref, g_ref,
                    rwh_ref, rwl_ref, rb_ref, tri_ref,
                    x1_ref, dest_ref, gate_ref, cnt_ref, xs_hbm,
                    hbuf, dvm, dsm, cvm, csm, zbuf, carry, sem_idx, sem_push, sem_zero):
    i = pl.program_id(0)
    n = pl.num_programs(0)
    slot = i % 2

    @pl.when(i == 0)
    def _():
        carry[...] = jnp.zeros_like(carry)

    mixed = (jnp.dot(yc_ref[...], wt_ref[...], preferred_element_type=F32)
             + jnp.dot(ya_ref[...], wb_ref[...], preferred_element_type=F32))
    x1 = x_ref[...] + g1_ref[0] * mixed
    x1_ref[...] = x1
    h2 = _modulated_norm(x1, g_ref[...], sc_ref[0], sh_ref[0])

    hi = h2.astype(BF16)
    lo = (h2 - hi.astype(F32)).astype(BF16)
    logits = (jnp.dot(hi, rwh_ref[...], preferred_element_type=F32)
              + jnp.dot(hi, rwl_ref[...], preferred_element_type=F32)
              + jnp.dot(lo, rwh_ref[...], preferred_element_type=F32)) + rb_ref[...]
    lt = logits.T[:N_EXPERTS, :]

    eid = lax.broadcasted_iota(I32, (N_EXPERTS, TM), 0)
    picks, onehots, exps = [], [], []
    top0 = None
    for kk in range(TOP_K):
        m = jnp.max(lt, axis=0, keepdims=True)
        am = jnp.min(jnp.where(lt == m, eid, N_EXPERTS), axis=0, keepdims=True)
        oh = eid == am
        lt = jnp.where(oh, NEG_INF, lt)
        if kk == 0:
            top0 = m
        picks.append(am)
        onehots.append(oh)
        exps.append(jnp.exp(m - top0))
    denom = exps[0] + exps[1] + exps[2] + exps[3]
    gate_ref[...] = jnp.concatenate(exps, axis=0) * (1.0 / denom)

    oh_all = (onehots[0].astype(F32) + onehots[1].astype(F32)
              + onehots[2].astype(F32) + onehots[3].astype(F32))
    before = jnp.dot(oh_all.astype(BF16), tri_ref[...], preferred_element_type=F32) + carry[:, 0:1]
    dests = []
    for kk in range(TOP_K):
        rank = jnp.sum(jnp.where(onehots[kk], before, 0.0), axis=0, keepdims=True)
        dests.append(picks[kk] * EXPERT_CAP + rank.astype(I32))
    dest = jnp.concatenate(dests, axis=0)
    dest_ref[...] = dest
    carry[...] = carry[...] + jnp.sum(oh_all, axis=1, keepdims=True)
    cnt_ref[...] = carry[...]

    dvm[...] = jnp.concatenate([dest, jnp.zeros((SUBLANES - TOP_K, TM), I32)], axis=0)
    idx_copy = pltpu.make_async_copy(dvm, dsm, sem_idx)
    idx_copy.start()

    hb = pl.multiple_of(slot * (TM * ROW_TILE), TM * ROW_TILE)
    _store_row_tiles(hbuf, hb, TM, h2)

    def wait_push(s):
        base = pl.multiple_of(s * (TM * ROW_TILE), TM * ROW_TILE)
        for _ in range(TOP_K):
            pltpu.make_async_copy(hbuf.at[pl.ds(base, TM * ROW_TILE), :],
                                  xs_hbm.at[pl.ds(0, TM * ROW_TILE), :], sem_push.at[s]).wait()

    @pl.when(i > 0)
    def _():
        wait_push(1 - slot)

    idx_copy.wait()
    per_iter = DMA_UNROLL // TOP_K

    def push_body(r, c):
        for u in range(DMA_UNROLL):
            t = r * per_iter + u // TOP_K
            d = dsm[u % TOP_K, t]
            _tile_copy(hbuf, slot * TM + t, xs_hbm, d, sem_push.at[slot]).start(priority=u % 2)
        return c

    lax.fori_loop(0, TM // per_iter, push_body, 0)

    @pl.when(i == n - 1)
    def _():
        wait_push(slot)
        zbuf[...] = jnp.zeros_like(zbuf)
        cvm[...] = carry[...].astype(I32)
        cnt_copy = pltpu.make_async_copy(cvm, csm, sem_idx)
        cnt_copy.start()
        cnt_copy.wait()

        def per_expert(e, c, wait):
            cnt = csm[e, 0]
            pad = (MOE_M - cnt % MOE_M) % MOE_M
            for b in range(ZERO_ROWS.bit_length()):
                size = (1 << b) * ROW_TILE
                start = (e * EXPERT_CAP + cnt + (pad & ((1 << b) - 1))) * ROW_TILE

                @pl.when(((pad >> b) & 1) == 1)
                def _():
                    cp = pltpu.make_async_copy(zbuf.at[pl.ds(0, size), :],
                                               xs_hbm.at[pl.ds(start, size), :], sem_zero)
                    if wait:
                        cp.wait()
                    else:
                        cp.start()
            return c

        lax.fori_loop(0, N_EXPERTS, lambda e, c: per_expert(e, c, False), 0)
        lax.fori_loop(0, N_EXPERTS, lambda e, c: per_expert(e, c, True), 0)


def _outproj(yc, ya, x, mod3, w_top, w_bot, norm_g, rw_hi, rw_lo, rb, tri):
    row = lambda i: (i, 0)
    col = lambda i: (0, i)
    const = lambda i: (0, 0)
    modspec = lambda j: pl.BlockSpec((1, 1, D_MODEL), lambda i: (i // TILES_PER_SEQ, 0, j))
    return pl.pallas_call(
        _outproj_kernel,
        grid=(TOKENS // TM,),
        in_specs=[
            pl.BlockSpec((TM, D_CONV), row),
            pl.BlockSpec((TM, D_ATTN), row),
            pl.BlockSpec((TM, D_MODEL), row),
            modspec(2),
            pl.BlockSpec((D_CONV, D_MODEL), const),
            pl.BlockSpec((D_ATTN, D_MODEL), const),
            modspec(3),
            modspec(4),
            pl.BlockSpec((1, D_MODEL), const),
            pl.BlockSpec((D_MODEL, LANES), const),
            pl.BlockSpec((D_MODEL, LANES), const),
            pl.BlockSpec((1, LANES), const),
            pl.BlockSpec((TM, TM), const),
        ],
        out_specs=[
            pl.BlockSpec((TM, D_MODEL), row),
            pl.BlockSpec((TOP_K, TM), col),
            pl.BlockSpec((TOP_K, TM), col),
            pl.BlockSpec((N_EXPERTS, LANES), const),
            pl.BlockSpec(memory_space=pl.ANY),
        ],
        out_shape=[
            jax.ShapeDtypeStruct((TOKENS, D_MODEL), F32),
            jax.ShapeDtypeStruct((TOP_K, TOKENS), I32),
            jax.ShapeDtypeStruct((TOP_K, TOKENS), F32),
            jax.ShapeDtypeStruct((N_EXPERTS, LANES), F32),
            jax.ShapeDtypeStruct((N_EXPERTS * EXPERT_CAP * ROW_TILE, LANES), F32),
        ],
        scratch_shapes=[
            pltpu.VMEM((2 * TM * ROW_TILE, LANES), F32),
            pltpu.VMEM((SUBLANES, TM), I32),
            pltpu.SMEM((SUBLANES, TM), I32),
            pltpu.VMEM((N_EXPERTS, LANES), I32),
            pltpu.SMEM((N_EXPERTS, LANES), I32),
            pltpu.VMEM((ZERO_ROWS * ROW_TILE, LANES), F32),
            pltpu.VMEM((N_EXPERTS, LANES), F32),
            pltpu.SemaphoreType.DMA(()),
            pltpu.SemaphoreType.DMA((2,)),
            pltpu.SemaphoreType.DMA(()),
        ],
        compiler_params=pltpu.CompilerParams(
            dimension_semantics=("arbitrary",), vmem_limit_bytes=VMEM_LIMIT),
        name="outproj_router",
    )(yc, ya, x, mod3, w_top, w_bot, mod3, mod3, norm_g, rw_hi, rw_lo, rb, tri)


def _expert_kernel(nblk_ref, xs_hbm, w1_ref, b1_ref, w2_ref, b2_ref, ys_hbm,
                   xin, yout, w1b, w2b, sem_in, sem_out):
    e = pl.program_id(0)
    nb = nblk_ref[e]
    blk = MOE_M * ROW_TILE
    first = e * (EXPERT_CAP * ROW_TILE)

    def in_copy(j, s, expert_first=first):
        return pltpu.make_async_copy(
            xs_hbm.at[pl.ds(pl.multiple_of(expert_first + j * blk, blk), blk), :],
            xin.at[pl.ds(pl.multiple_of(s * blk, blk), blk), :], sem_in.at[s])

    def out_copy(j, s):
        return pltpu.make_async_copy(
            yout.at[pl.ds(pl.multiple_of(s * blk, blk), blk), :],
            ys_hbm.at[pl.ds(pl.multiple_of(first + j * blk, blk), blk), :], sem_out.at[s])

    @pl.when(jnp.logical_and(e == 0, nb > 0))
    def _():
        in_copy(0, 0).start(priority=1)

    @pl.when(nb > 0)
    def _():
        w1b[...] = w1_ref[0, 0].astype(BF16)
        w2b[...] = w2_ref[0, 0].astype(BF16)

        def body(j, carry):
            s = j % 2
            base = pl.multiple_of(s * blk, blk)

            @pl.when(j + 1 < nb)
            def _():
                in_copy(j + 1, 1 - s).start(priority=1)

            in_copy(j, s).wait()
            xs = jnp.concatenate(
                [_load_row_tile_chunk(xin, base, MOE_M, c) for c in range(ROW_TILE)], axis=1).astype(BF16)
            u = jnp.dot(xs, w1b[...], preferred_element_type=F32) + b1_ref[0, 0]
            glu = jnp.minimum(u[:, :D_EXPERT], SWIGLU_LIMIT)
            lin = jnp.clip(u[:, D_EXPERT:], -SWIGLU_LIMIT, SWIGLU_LIMIT)
            act = glu * _sigmoid(SWIGLU_ALPHA * glu) * (lin + 1.0)
            y = jnp.dot(act.astype(BF16), w2b[...], preferred_element_type=F32) + b2_ref[0, 0]

            @pl.when(j >= 2)
            def _():
                out_copy(j - 2, s).wait()

            _store_row_tiles(yout, base, MOE_M, y)
            out_copy(j, s).start(priority=1)
            return carry

        lax.fori_loop(0, nb, body, 0)

        @pl.when(nb >= 2)
        def _():
            out_copy(nb - 2, nb % 2).wait()

        out_copy(nb - 1, (nb - 1) % 2).wait()

    nxt = jnp.minimum(e + 1, N_EXPERTS - 1)

    @pl.when(jnp.logical_and(e + 1 < N_EXPERTS, nblk_ref[nxt] > 0))
    def _():
        in_copy(0, 0, nxt * (EXPERT_CAP * ROW_TILE)).start(priority=1)


def _experts(layer, nblk, xs, w1, b1, w2, b2):
    wmap = lambda e, nb: (layer, e, 0, 0)
    blk = MOE_M * ROW_TILE
    grid_spec = pltpu.PrefetchScalarGridSpec(
        num_scalar_prefetch=1,
        grid=(N_EXPERTS,),
        in_specs=[
            pl.BlockSpec(memory_space=pl.ANY),
            pl.BlockSpec((1, 1, D_MODEL, 2 * D_EXPERT), wmap),
            pl.BlockSpec((1, 1, 1, 2 * D_EXPERT), wmap),
            pl.BlockSpec((1, 1, D_EXPERT, D_MODEL), wmap),
            pl.BlockSpec((1, 1, 1, D_MODEL), wmap),
        ],
        out_specs=pl.BlockSpec(memory_space=pl.ANY),
        scratch_shapes=[
            pltpu.VMEM((2 * blk, LANES), F32),
            pltpu.VMEM((2 * blk, LANES), F32),
            pltpu.VMEM((D_MODEL, 2 * D_EXPERT), BF16),
            pltpu.VMEM((D_EXPERT, D_MODEL), BF16),
            pltpu.SemaphoreType.DMA((2,)),
            pltpu.SemaphoreType.DMA((2,)),
        ],
    )
    return pl.pallas_call(
        _expert_kernel,
        grid_spec=grid_spec,
        out_shape=jax.ShapeDtypeStruct((N_EXPERTS * EXPERT_CAP * ROW_TILE, LANES), F32),
        compiler_params=pltpu.CompilerParams(
            dimension_semantics=("arbitrary",), vmem_limit_bytes=VMEM_LIMIT),
        name="experts",
    )(nblk, xs, w1, b1.reshape(DEPTH, N_EXPERTS, 1, 2 * D_EXPERT), w2,
      b2.reshape(DEPTH, N_EXPERTS, 1, D_MODEL))


def _gather_combine(tile, idx0_ref, idxn_ref, ys_hbm, x1_ref, g2_ref, gate_ref, o_ref, bufs, sem,
                    then=None):
    i = pl.program_id(0)
    n = pl.num_programs(0)
    rows = tile * TOP_K

    def wait_gather(s):
        pltpu.make_async_copy(ys_hbm.at[pl.ds(0, rows * ROW_TILE), :], bufs[s], sem.at[s]).wait()

    @pl.when(i == 0)
    def _():
        def body(r, c):
            for u in range(DMA_UNROLL):
                j = r * DMA_UNROLL + u
                _tile_copy(ys_hbm, idx0_ref[0, 0, j], bufs[0], j, sem.at[0]).start(priority=u % 2)
            return c

        lax.fori_loop(0, rows // DMA_UNROLL, body, 0)

    def step(cur, nxt):
        wait_gather(cur)
        gates = gate_ref[...]
        g2 = g2_ref[0]
        for c in range(ROW_TILE):
            cs = slice(c * LANES, (c + 1) * LANES)
            acc = jnp.zeros((tile, LANES), F32)
            for kk in range(TOP_K):
                acc = acc + gates[:, kk:kk + 1] * _load_row_tile_chunk(
                    bufs[cur], kk * tile * ROW_TILE, tile, c)
            o_ref[:, cs] = x1_ref[:, cs] + g2[:, cs] * acc
        if then is not None:
            then()
        for j in range(rows):
            _tile_copy(ys_hbm, idxn_ref[0, 0, j], bufs[nxt], j, sem.at[nxt]).start(priority=j % 2)

        @pl.when(i == n - 1)
        def _():
            wait_gather(nxt)

    @pl.when(i % 2 == 0)
    def _():
        step(0, 1)

    @pl.when(i % 2 == 1)
    def _():
        step(1, 0)


def _combine_kernel(idx0_ref, idxn_ref, ys_hbm, x1_ref, g2_ref, gate_ref, o_ref, buf0, buf1, sem):
    _gather_combine(TMC, idx0_ref, idxn_ref, ys_hbm, x1_ref, g2_ref, gate_ref, o_ref, (buf0, buf1), sem)


def _combine_inproj_kernel(idx0_ref, idxn_ref, ys_hbm, x1_ref, g2_ref, gate_ref,
                           sh_ref, sc_ref, g_ref, w_ref, seg_ref, qg_ref, kg_ref,
                           xn_ref, ug_ref, q_ref, k_ref, v_ref, buf0, buf1, sem):
    def inproj():
        _inproj_body(xn_ref[...], sh_ref, sc_ref, g_ref, w_ref, seg_ref, qg_ref, kg_ref,
                     ug_ref, q_ref, k_ref, v_ref)

    _gather_combine(TM, idx0_ref, idxn_ref, ys_hbm, x1_ref, g2_ref, gate_ref, xn_ref, (buf0, buf1), sem,
                    then=inproj)


def _choice_major(dest, tile):
    return dest.reshape(TOP_K, TOKENS // tile, tile).transpose(1, 0, 2).reshape(
        TOKENS // tile, 1, TOP_K * tile)


def _combine_inproj(dest, ys, x1, mod3_prev, gates, mod3, norm_g, w_in_bf, seg, qg, kg):
    n_tiles = TOKENS // TM
    rows = TM * TOP_K
    row = lambda i: (i, 0)
    const = lambda i: (0, 0)
    modspec = lambda j: pl.BlockSpec((1, 1, D_MODEL), lambda i: (i // TILES_PER_SEQ, 0, j))
    half_sd = jax.ShapeDtypeStruct((TOKENS, D_CONV), BF16)
    return pl.pallas_call(
        _combine_inproj_kernel,
        grid=(n_tiles,),
        in_specs=[
            pl.BlockSpec((1, 1, rows), lambda i: (0, 0, 0), memory_space=pltpu.SMEM),
            pl.BlockSpec((1, 1, rows), lambda i: (jnp.minimum(i + 1, n_tiles - 1), 0, 0),
                         memory_space=pltpu.SMEM),
            pl.BlockSpec(memory_space=pl.ANY),
            pl.BlockSpec((TM, D_MODEL), row),
            modspec(5),
            pl.BlockSpec((TM, TOP_K), row),
            modspec(0),
            modspec(1),
            pl.BlockSpec((1, D_MODEL), const),
            pl.BlockSpec((D_MODEL, D_IN), const),
            pl.BlockSpec((D_ATTN, D_ATTN), const),
            pl.BlockSpec((1, D_ATTN), const),
            pl.BlockSpec((1, D_ATTN), const),
        ],
        out_specs=[pl.BlockSpec((TM, D_MODEL), row)] + [pl.BlockSpec((TM, D_CONV), row)] * 4,
        out_shape=[jax.ShapeDtypeStruct((TOKENS, D_MODEL), F32)] + [half_sd] * 4,
        scratch_shapes=[
            pltpu.VMEM((rows * ROW_TILE, LANES), F32),
            pltpu.VMEM((rows * ROW_TILE, LANES), F32),
            pltpu.SemaphoreType.DMA((2,)),
        ],
        compiler_params=pltpu.CompilerParams(
            dimension_semantics=("arbitrary",), vmem_limit_bytes=VMEM_LIMIT),
        name="combine_inproj",
    )(_choice_major(dest, TM), _choice_major(dest, TM), ys, x1, mod3_prev, gates,
      mod3, mod3, norm_g, w_in_bf, seg, qg, kg)


def _combine(dest3, ys, x1, mod3, gates):
    n_tiles = TOKENS // TMC
    tiles_per_seq = SEQ // TMC
    rows = TMC * TOP_K
    return pl.pallas_call(
        _combine_kernel,
        grid=(n_tiles,),
        in_specs=[
            pl.BlockSpec((1, 1, rows), lambda i: (0, 0, 0), memory_space=pltpu.SMEM),
            pl.BlockSpec((1, 1, rows), lambda i: (jnp.minimum(i + 1, n_tiles - 1), 0, 0),
                         memory_space=pltpu.SMEM),
            pl.BlockSpec(memory_space=pl.ANY),
            pl.BlockSpec((TMC, D_MODEL), lambda i: (i, 0)),
            pl.BlockSpec((1, 1, D_MODEL), lambda i: (i // tiles_per_seq, 0, 5)),
            pl.BlockSpec((TMC, TOP_K), lambda i: (i, 0)),
        ],
        out_specs=pl.BlockSpec((TMC, D_MODEL), lambda i: (i, 0)),
        out_shape=jax.ShapeDtypeStruct((TOKENS, D_MODEL), F32),
        scratch_shapes=[
            pltpu.VMEM((rows * ROW_TILE, LANES), F32),
            pltpu.VMEM((rows * ROW_TILE, LANES), F32),
            pltpu.SemaphoreType.DMA((2,)),
        ],
        compiler_params=pltpu.CompilerParams(
            dimension_semantics=("arbitrary",), vmem_limit_bytes=VMEM_LIMIT),
        name="moe_combine",
    )(dest3, dest3, ys, x1, mod3, gates)


_I = np.arange(Q_PAIR)[:, None]
_K = np.arange(KEY_WIN)[None, :]
_BAND_OK = (_K // CHUNK >= _I // CHUNK) & (_K // CHUNK <= _I // CHUNK + LEFT_CHUNKS)
_SEG = np.kron(np.eye(N_HEADS), np.full((HEAD_DIM, HEAD_DIM), 1.0 / HEAD_DIM)).astype(np.float32)
_TRI = (np.arange(TM)[:, None] < np.arange(TM)[None, :]).astype(np.float32)


def _bias_tables(rel_bias):
    n_far = KEY_WIN - 1 + Q_PAIR - 2 * REL_CLIP
    ext = jnp.concatenate(
        [rel_bias[..., 1:], jnp.repeat(rel_bias[..., 2 * REL_CLIP:], n_far, axis=-1)], axis=-1)
    rev = ext[..., ::-1]
    period = KEY_WIN + Q_PAIR
    ring = jnp.concatenate(
        [rev[..., Q_PAIR - 1:], jnp.zeros(rev.shape[:-1] + (1,), rev.dtype), rev[..., :Q_PAIR - 1]],
        axis=-1)
    lead = rel_bias.shape[:-1]
    skew = jnp.tile(ring, Q_PAIR)[..., :Q_PAIR * (period - 1)].reshape(lead + (Q_PAIR, period - 1))
    table = jnp.where(_BAND_OK, skew[..., :KEY_WIN], NEG_INF).astype(F32)
    return table.reshape(DEPTH, N_HEADS // 2, 2, Q_PAIR, KEY_WIN).transpose(0, 1, 4, 2, 3).reshape(
        DEPTH, N_HEADS // 2, KEY_WIN, 2 * Q_PAIR)


def kernel(x, c, ada_w, ada_b, norm1_g, w_in, conv_w, conv_b, conv_ln_g, conv_ln_b, q_norm_g,
           k_norm_g, rel_bias, w_out, norm2_g, router_w, router_b, exp_w1, exp_b1, exp_w2, exp_b2):
    xf = x.reshape(TOKENS, D_MODEL)
    mod = _ada_mod(c, ada_w, ada_b)
    seg = jnp.asarray(_SEG, BF16)
    tri = jnp.asarray(_TRI, BF16)
    pad_e = ((0, 0), (0, LANES - N_EXPERTS))
    bias_tables = _bias_tables(rel_bias)
    moe = None
    for l in range(DEPTH):
        mod3 = mod[l].reshape(BATCH, 1, 6 * D_MODEL)
        qg = (jnp.tile(q_norm_g[l], N_HEADS) * (HEAD_DIM ** -0.5)).reshape(1, D_ATTN)
        kg = jnp.tile(k_norm_g[l], N_HEADS).reshape(1, D_ATTN)
        inproj_params = (mod3, norm1_g[l].reshape(1, D_MODEL), w_in[l].astype(BF16), seg, qg, kg)
        if moe is None:
            ug, q, k, v = _inproj(xf, *inproj_params)
        else:
            xf, ug, q, k, v = _combine_inproj(*moe, *inproj_params)
        y_conv = _conv(ug, conv_w[l], conv_b[l].reshape(1, D_CONV),
                       conv_ln_g[l].reshape(1, D_CONV), conv_ln_b[l].reshape(1, D_CONV))
        y_attn = _attention(q, k, v, bias_tables[l])
        w_out_bf = w_out[l].astype(BF16)
        rw_hi = router_w[l].astype(BF16)
        rw_lo = (router_w[l] - rw_hi.astype(F32)).astype(BF16)
        x1, dest, gates, cnt, xs = _outproj(
            y_conv, y_attn, xf, mod3, w_out_bf[:D_CONV], w_out_bf[D_CONV:],
            norm2_g[l].reshape(1, D_MODEL), jnp.pad(rw_hi, pad_e), jnp.pad(rw_lo, pad_e),
            jnp.pad(router_b[l].reshape(1, N_EXPERTS), pad_e), tri)
        nblk = (cnt[:, 0].astype(I32) + MOE_M - 1) // MOE_M
        ys = _experts(l, nblk, xs, exp_w1, exp_b1, exp_w2, exp_b2)
        moe = (dest, ys, x1, mod3, gates.T)
    dest, ys, x1, mod3, gates_t = moe
    xf = _combine(_choice_major(dest, TMC), ys, x1, mod3, gates_t)
    return xf.reshape(BATCH, SEQ, D_MODEL)
```

```python
import functools

import jax
import jax.numpy as jnp
import numpy as np
from jax import lax
from jax.experimental import pallas as pl
from jax.experimental.pallas import tpu as pltpu

D_MODEL = 1024
BATCH = 8
SEQ = 2048
DEPTH = 4
TOKENS = BATCH * SEQ

CHUNK = 64
D_CONV = D_MODEL // 2
D_ATTN = D_MODEL - D_CONV
N_HEADS = 8
HEAD_DIM = D_ATTN // N_HEADS
CONV_KERNEL = 31
LEFT_CHUNKS = 8
REL_CLIP = 128
N_EXPERTS = 32
TOP_K = 4
D_EXPERT = D_MODEL
SWIGLU_ALPHA = 1.702
SWIGLU_LIMIT = 7.0
EPS = 1e-6
D_IN = 2 * D_CONV + 3 * D_ATTN

F32 = jnp.float32
BF16 = jnp.bfloat16
I32 = jnp.int32
NEG_INF = float("-inf")

LANES = 128
SUBLANES = 8
TM = 512
TILES_PER_SEQ = SEQ // TM
Q_PAIR = 2 * CHUNK
KEY_WIN = (LEFT_CHUNKS + 2) * CHUNK
PAIRS_PER_TILE = TM // Q_PAIR
HALO = 32
CONV_ROWS = 32
CONV_CHUNKS_PER_ITER = 4
MOE_M = 512
ZERO_ROWS = MOE_M // 2
N_BLOCKS = TOKENS * TOP_K // MOE_M + N_EXPERTS
EXPERT_CAP = TOKENS
ROW_TILE = D_MODEL // LANES
TMC = 256
DMA_UNROLL = 8

VMEM_LIMIT = 56 * 1024 * 1024


def _sigmoid(x):
    return 1.0 / (1.0 + jnp.exp(-x))


def _ada_kernel(c_ref, w_ref, b_ref, o_ref):
    c = c_ref[...]
    ca = (c * _sigmoid(c)).astype(BF16)
    o_ref[0] = jnp.dot(ca, w_ref[0].astype(BF16), preferred_element_type=F32) + b_ref[0]


def _ada_mod(c, ada_w, ada_b):
    n_tile = 1536
    return pl.pallas_call(
        _ada_kernel,
        grid=(DEPTH, 6 * D_MODEL // n_tile),
        in_specs=[
            pl.BlockSpec((BATCH, D_MODEL), lambda l, n: (0, 0)),
            pl.BlockSpec((1, D_MODEL, n_tile), lambda l, n: (l, 0, n)),
            pl.BlockSpec((1, 1, n_tile), lambda l, n: (l, 0, n)),
        ],
        out_specs=pl.BlockSpec((1, BATCH, n_tile), lambda l, n: (l, 0, n)),
        out_shape=jax.ShapeDtypeStruct((DEPTH, BATCH, 6 * D_MODEL), F32),
        compiler_params=pltpu.CompilerParams(
            dimension_semantics=("arbitrary", "arbitrary"), vmem_limit_bytes=VMEM_LIMIT),
        name="ada_mod",
    )(c, ada_w, ada_b.reshape(DEPTH, 1, 6 * D_MODEL))


def _modulated_norm(x, g, sc, sh):
    ms = jnp.mean(x * x, axis=-1, keepdims=True)
    return x * lax.rsqrt(ms + EPS) * g * (1.0 + sc) + sh


def _inproj_kernel(x_ref, *refs):
    _inproj_body(x_ref[...], *refs)


def _inproj_body(x, sh_ref, sc_ref, g_ref, w_ref, seg_ref, qg_ref, kg_ref, ug_ref, q_ref, k_ref, v_ref):
    h = _modulated_norm(x, g_ref[...], sc_ref[0], sh_ref[0])
    u = jnp.dot(h.astype(BF16), w_ref[...], preferred_element_type=F32)
    a = u[:, :D_CONV]
    gt = u[:, D_CONV:2 * D_CONV]
    ug_ref[...] = (a * _sigmoid(gt)).astype(BF16)
    o = 2 * D_CONV
    q = u[:, o:o + D_ATTN]
    k = u[:, o + D_ATTN:o + 2 * D_ATTN]
    v = u[:, o + 2 * D_ATTN:]
    qms = jnp.dot((q * q).astype(BF16), seg_ref[...], preferred_element_type=F32)
    kms = jnp.dot((k * k).astype(BF16), seg_ref[...], preferred_element_type=F32)
    q_ref[...] = (q * lax.rsqrt(qms + EPS) * qg_ref[...]).astype(BF16)
    k_ref[...] = (k * lax.rsqrt(kms + EPS) * kg_ref[...]).astype(BF16)
    v_ref[...] = v.astype(BF16)


def _inproj(x, mod3, norm_g, w_in_bf, seg, qg, kg):
    row = lambda i: (i, 0)
    const = lambda i: (0, 0)
    out_sd = jax.ShapeDtypeStruct((TOKENS, D_CONV), BF16)
    return pl.pallas_call(
        _inproj_kernel,
        grid=(TOKENS // TM,),
        in_specs=[
            pl.BlockSpec((TM, D_MODEL), row),
            pl.BlockSpec((1, 1, D_MODEL), lambda i: (i // TILES_PER_SEQ, 0, 0)),
            pl.BlockSpec((1, 1, D_MODEL), lambda i: (i // TILES_PER_SEQ, 0, 1)),
            pl.BlockSpec((1, D_MODEL), const),
            pl.BlockSpec((D_MODEL, D_IN), const),
            pl.BlockSpec((D_ATTN, D_ATTN), const),
            pl.BlockSpec((1, D_ATTN), const),
            pl.BlockSpec((1, D_ATTN), const),
        ],
        out_specs=[pl.BlockSpec((TM, D_CONV), row)] * 4,
        out_shape=[out_sd] * 4,
        compiler_params=pltpu.CompilerParams(
            dimension_semantics=("arbitrary",), vmem_limit_bytes=VMEM_LIMIT),
        name="inproj",
    )(x, mod3, mod3, norm_g, w_in_bf, seg, qg, kg)


def _conv_kernel(cur_ref, halo_ref, w_ref, b_ref, lg_ref, lb_ref, o_ref, xe_ref):
    i = pl.program_id(0)
    seq_start = (i % TILES_PER_SEQ) == 0
    halo = halo_ref[...].astype(F32)
    xe_ref[0, 0:HALO, :] = jnp.where(seq_start, 0.0, halo)
    xe_ref[0, HALO:, :] = cur_ref[...].astype(F32)
    ext = HALO + TM
    for s in range(1, SUBLANES):
        xe_ref[s, 0:ext - SUBLANES, :] = xe_ref[0, s:ext - SUBLANES + s, :]
    n_lane = D_CONV // LANES
    first_tap = HALO - (CONV_KERNEL - 1)

    lane_slices = [slice(c * LANES, (c + 1) * LANES) for c in range(n_lane)]

    def row_chunk(r0):
        accs = [jnp.zeros((CONV_ROWS, LANES), F32) + b_ref[:, cs] for cs in lane_slices]
        for j in range(CONV_KERNEL):
            shift = (first_tap + j) % SUBLANES
            aligned = first_tap + j - shift
            for c, cs in enumerate(lane_slices):
                accs[c] = accs[c] + w_ref[j:j + 1, cs] * xe_ref[shift, pl.ds(r0 + aligned, CONV_ROWS), cs]
        tot = accs[0]
        for c in range(1, n_lane):
            tot = tot + accs[c]
        mu = tot.sum(axis=-1, keepdims=True) * (1.0 / D_CONV)
        cen = [a - mu for a in accs]
        sq = cen[0] * cen[0]
        for c in range(1, n_lane):
            sq = sq + cen[c] * cen[c]
        inv = lax.rsqrt(sq.sum(axis=-1, keepdims=True) * (1.0 / D_CONV) + EPS)
        for c, cs in enumerate(lane_slices):
            y = cen[c] * inv * lg_ref[:, cs] + lb_ref[:, cs]
            o_ref[pl.ds(r0, CONV_ROWS), cs] = (y * _sigmoid(y)).astype(BF16)

    def body(rc, carry):
        for sub in range(CONV_CHUNKS_PER_ITER):
            row_chunk(pl.multiple_of((rc * CONV_CHUNKS_PER_ITER + sub) * CONV_ROWS, CONV_ROWS))
        return carry

    lax.fori_loop(0, TM // (CONV_ROWS * CONV_CHUNKS_PER_ITER), body, 0)


def _conv(ug, conv_w, conv_b, ln_g, ln_b):
    const = lambda i: (0, 0)
    halo_per_tile = TM // HALO
    return pl.pallas_call(
        _conv_kernel,
        grid=(TOKENS // TM,),
        in_specs=[
            pl.BlockSpec((TM, D_CONV), lambda i: (i, 0)),
            pl.BlockSpec((HALO, D_CONV), lambda i: (jnp.maximum(i * halo_per_tile - 1, 0), 0)),
            pl.BlockSpec((CONV_KERNEL, D_CONV), const),
            pl.BlockSpec((1, D_CONV), const),
            pl.BlockSpec((1, D_CONV), const),
            pl.BlockSpec((1, D_CONV), const),
        ],
        out_specs=pl.BlockSpec((TM, D_CONV), lambda i: (i, 0)),
        out_shape=jax.ShapeDtypeStruct((TOKENS, D_CONV), BF16),
        scratch_shapes=[pltpu.VMEM((SUBLANES, HALO + TM, D_CONV), F32)],
        compiler_params=pltpu.CompilerParams(
            dimension_semantics=("arbitrary",), vmem_limit_bytes=VMEM_LIMIT),
        name="conv_mixer",
    )(ug, ug, conv_w, conv_b, ln_g, ln_b)


def _attn_kernel(q_ref, kp_ref, kc_ref, vp_ref, vc_ref, bias_ref, o_ref, kw_ref, vt_ref):
    i = pl.program_id(0)
    seq_start = (i % TILES_PER_SEQ) == 0
    kw_ref[0:TM, :] = kp_ref[...]
    kw_ref[TM:, :] = kc_ref[...]
    vt_ref[:, 0:TM] = vp_ref[...].T
    vt_ref[:, TM:] = vc_ref[...].T
    pair = 2 * HEAD_DIM
    low = lax.broadcasted_iota(jnp.int32, (Q_PAIR, pair), 1) < HEAD_DIM

    def pair_block(cp, hp, k_lo):
        r0 = cp * Q_PAIR
        ps = slice(hp * pair, (hp + 1) * pair)
        q2 = q_ref[r0:r0 + Q_PAIR, ps]
        zero = jnp.zeros_like(q2)
        qm = jnp.concatenate([jnp.where(low, q2, zero), jnp.where(low, zero, q2)], axis=0)
        k2 = kw_ref[r0 + k_lo:r0 + KEY_WIN, ps]
        st = lax.dot_general(k2, qm, (((1,), (1,)), ((), ())), preferred_element_type=F32)
        st = st + bias_ref[hp, k_lo:, :]
        m = jnp.max(st, axis=0, keepdims=True)
        e = jnp.exp(st - m)
        l = jnp.sum(e, axis=0, keepdims=True)
        ot = jnp.dot(vt_ref[ps, r0 + k_lo:r0 + KEY_WIN], e.astype(BF16),
                     preferred_element_type=F32) * (1.0 / l)
        o2t = jnp.concatenate([ot[:HEAD_DIM, :Q_PAIR], ot[HEAD_DIM:, Q_PAIR:]], axis=0)
        o_ref[r0:r0 + Q_PAIR, ps] = o2t.T.astype(BF16)

    @pl.when(seq_start)
    def _():
        for cp in range(PAIRS_PER_TILE):
            for hp in range(N_HEADS // 2):
                pair_block(cp, hp, TM - cp * Q_PAIR)

    @pl.when(jnp.logical_not(seq_start))
    def _():
        for cp in range(PAIRS_PER_TILE):
            for hp in range(N_HEADS // 2):
                pair_block(cp, hp, 0)


def _attention(q, k, v, bias2):
    cur = lambda i: (i, 0)
    prev = lambda i: (jnp.maximum(i - 1, 0), 0)
    blk = (TM, D_ATTN)
    return pl.pallas_call(
        _attn_kernel,
        grid=(TOKENS // TM,),
        in_specs=[
            pl.BlockSpec(blk, cur),
            pl.BlockSpec(blk, prev), pl.BlockSpec(blk, cur),
            pl.BlockSpec(blk, prev), pl.BlockSpec(blk, cur),
            pl.BlockSpec((N_HEADS // 2, KEY_WIN, 2 * Q_PAIR), lambda i: (0, 0, 0)),
        ],
        out_specs=pl.BlockSpec(blk, cur),
        out_shape=jax.ShapeDtypeStruct((TOKENS, D_ATTN), BF16),
        scratch_shapes=[pltpu.VMEM((2 * TM, D_ATTN), BF16), pltpu.VMEM((D_ATTN, 2 * TM), BF16)],
        compiler_params=pltpu.CompilerParams(
            dimension_semantics=("arbitrary",), vmem_limit_bytes=VMEM_LIMIT),
        name="chunk_attention",
    )(q, k, k, v, v, bias2)


def _store_row_tiles(ref, base, rows, value):
    for c in range(ROW_TILE):
        ref[pl.ds(base + c, rows, stride=ROW_TILE), :] = value[:, c * LANES:(c + 1) * LANES]


def _load_row_tile_chunk(ref, base, rows, c):
    return ref[pl.ds(base + c, rows, stride=ROW_TILE), :]


def _tile_copy(src_ref, src_row, dst_ref, dst_row, sem):
    return pltpu.make_async_copy(
        src_ref.at[pl.ds(src_row * ROW_TILE, ROW_TILE), :],
        dst_ref.at[pl.ds(dst_row * ROW_TILE, ROW_TILE), :], sem)


def _outproj_kernel(yc_ref, ya_ref, x_ref, g1_ref, wt_ref, wb_ref, sh_ref, sc_ref, g_ref,
                    rwh_ref, rwl_ref, rb_ref, tri_ref,
                    x1_ref, dest_ref, gate_ref, cnt_ref, xs_hbm,
                    hbuf, dvm, dsm, cvm, csm, zbuf, carry, sem_idx, sem_push, sem_zero):
    i = pl.program_id(0)
    n = pl.num_programs(0)
    slot = i % 2

    @pl.when(i == 0)
    def _():
        carry[...] = jnp.zeros_like(carry)

    mixed = (jnp.dot(yc_ref[...], wt_ref[...], preferred_element_type=F32)
             + jnp.dot(ya_ref[...], wb_ref[...], preferred_element_type=F32))
    x1 = x_ref[...] + g1_ref[0] * mixed
    x1_ref[...] = x1
    h2 = _modulated_norm(x1, g_ref[...], sc_ref[0], sh_ref[0])

    hi = h2.astype(BF16)
    lo = (h2 - hi.astype(F32)).astype(BF16)
    logits = (jnp.dot(hi, rwh_ref[...], preferred_element_type=F32)
              + jnp.dot(hi, rwl_ref[...], preferred_element_type=F32)
              + jnp.dot(lo, rwh_ref[...], preferred_element_type=F32)) + rb_ref[...]
    lt = logits.T[:N_EXPERTS, :]

    eid = lax.broadcasted_iota(I32, (N_EXPERTS, TM), 0)
    picks, onehots, exps = [], [], []
    top0 = None
    for kk in range(TOP_K):
        m = jnp.max(lt, axis=0, keepdims=True)
        am = jnp.min(jnp.where(lt == m, eid, N_EXPERTS), axis=0, keepdims=True)
        oh = eid == am
        lt = jnp.where(oh, NEG_INF, lt)
        if kk == 0:
            top0 = m
        picks.append(am)
        onehots.append(oh)
        exps.append(jnp.exp(m - top0))
    denom = exps[0] + exps[1] + exps[2] + exps[3]
    gate_ref[...] = jnp.concatenate(exps, axis=0) * (1.0 / denom)

    oh_all = (onehots[0].astype(F32) + onehots[1].astype(F32)
              + onehots[2].astype(F32) + onehots[3].astype(F32))
    before = jnp.dot(oh_all.astype(BF16), tri_ref[...], preferred_element_type=F32) + carry[:, 0:1]
    dests = []
    for kk in range(TOP_K):
        rank = jnp.sum(jnp.where(onehots[kk], before, 0.0), axis=0, keepdims=True)
        dests.append(picks[kk] * EXPERT_CAP + rank.astype(I32))
    dest = jnp.concatenate(dests, axis=0)
    dest_ref[...] = dest
    carry[...] = carry[...] + jnp.sum(oh_all, axis=1, keepdims=True)
    cnt_ref[...] = carry[...]

    dvm[...] = jnp.concatenate([dest, jnp.zeros((SUBLANES - TOP_K, TM), I32)], axis=0)
    idx_copy = pltpu.make_async_copy(dvm, dsm, sem_idx)
    idx_copy.start()

    hb = pl.multiple_of(slot * (TM * ROW_TILE), TM * ROW_TILE)
    _store_row_tiles(hbuf, hb, TM, h2)

    def wait_push(s):
        base = pl.multiple_of(s * (TM * ROW_TILE), TM * ROW_TILE)
        for _ in range(TOP_K):
            pltpu.make_async_copy(hbuf.at[pl.ds(base, TM * ROW_TILE), :],
                                  xs_hbm.at[pl.ds(0, TM * ROW_TILE), :], sem_push.at[s]).wait()

    @pl.when(i > 0)
    def _():
        wait_push(1 - slot)

    idx_copy.wait()
    per_iter = DMA_UNROLL // TOP_K

    def push_body(r, c):
        for u in range(DMA_UNROLL):
            t = r * per_iter + u // TOP_K
            d = dsm[u % TOP_K, t]
            _tile_copy(hbuf, slot * TM + t, xs_hbm, d, sem_push.at[slot]).start(priority=u % 2)
        return c

    lax.fori_loop(0, TM // per_iter, push_body, 0)

    @pl.when(i == n - 1)
    def _():
        wait_push(slot)
        zbuf[...] = jnp.zeros_like(zbuf)
        cvm[...] = carry[...].astype(I32)
        cnt_copy = pltpu.make_async_copy(cvm, csm, sem_idx)
        cnt_copy.start()
        cnt_copy.wait()

        def per_expert(e, c, wait):
            cnt = csm[e, 0]
            pad = (MOE_M - cnt % MOE_M) % MOE_M
            for b in range(ZERO_ROWS.bit_length()):
                size = (1 << b) * ROW_TILE
                start = (e * EXPERT_CAP + cnt + (pad & ((1 << b) - 1))) * ROW_TILE

                @pl.when(((pad >> b) & 1) == 1)
                def _():
                    cp = pltpu.make_async_copy(zbuf.at[pl.ds(0, size), :],
                                               xs_hbm.at[pl.ds(start, size), :], sem_zero)
                    if wait:
                        cp.wait()
                    else:
                        cp.start()
            return c

        lax.fori_loop(0, N_EXPERTS, lambda e, c: per_expert(e, c, False), 0)
        lax.fori_loop(0, N_EXPERTS, lambda e, c: per_expert(e, c, True), 0)


def _outproj(yc, ya, x, mod3, w_top, w_bot, norm_g, rw_hi, rw_lo, rb, tri):
    row = lambda i: (i, 0)
    col = lambda i: (0, i)
    const = lambda i: (0, 0)
    modspec = lambda j: pl.BlockSpec((1, 1, D_MODEL), lambda i: (i // TILES_PER_SEQ, 0, j))
    return pl.pallas_call(
        _outproj_kernel,
        grid=(TOKENS // TM,),
        in_specs=[
            pl.BlockSpec((TM, D_CONV), row),
            pl.BlockSpec((TM, D_ATTN), row),
            pl.BlockSpec((TM, D_MODEL), row),
            modspec(2),
            pl.BlockSpec((D_CONV, D_MODEL), const),
            pl.BlockSpec((D_ATTN, D_MODEL), const),
            modspec(3),
            modspec(4),
            pl.BlockSpec((1, D_MODEL), const),
            pl.BlockSpec((D_MODEL, LANES), const),
            pl.BlockSpec((D_MODEL, LANES), const),
            pl.BlockSpec((1, LANES), const),
            pl.BlockSpec((TM, TM), const),
        ],
        out_specs=[
            pl.BlockSpec((TM, D_MODEL), row),
            pl.BlockSpec((TOP_K, TM), col),
            pl.BlockSpec((TOP_K, TM), col),
            pl.BlockSpec((N_EXPERTS, LANES), const),
            pl.BlockSpec(memory_space=pl.ANY),
        ],
        out_shape=[
            jax.ShapeDtypeStruct((TOKENS, D_MODEL), F32),
            jax.ShapeDtypeStruct((TOP_K, TOKENS), I32),
            jax.ShapeDtypeStruct((TOP_K, TOKENS), F32),
            jax.ShapeDtypeStruct((N_EXPERTS, LANES), F32),
            jax.ShapeDtypeStruct((N_EXPERTS * EXPERT_CAP * ROW_TILE, LANES), F32),
        ],
        scratch_shapes=[
            pltpu.VMEM((2 * TM * ROW_TILE, LANES), F32),
            pltpu.VMEM((SUBLANES, TM), I32),
            pltpu.SMEM((SUBLANES, TM), I32),
            pltpu.VMEM((N_EXPERTS, LANES), I32),
            pltpu.SMEM((N_EXPERTS, LANES), I32),
            pltpu.VMEM((ZERO_ROWS * ROW_TILE, LANES), F32),
            pltpu.VMEM((N_EXPERTS, LANES), F32),
            pltpu.SemaphoreType.DMA(()),
            pltpu.SemaphoreType.DMA((2,)),
            pltpu.SemaphoreType.DMA(()),
        ],
        compiler_params=pltpu.CompilerParams(
            dimension_semantics=("arbitrary",), vmem_limit_bytes=VMEM_LIMIT),
        name="outproj_router",
    )(yc, ya, x, mod3, w_top, w_bot, mod3, mod3, norm_g, rw_hi, rw_lo, rb, tri)


def _expert_kernel(layer, be_ref, br_ref, par_ref, nx_ref, nu_ref,
                   x_ref, w1_hbm, b1_ref, w2_hbm, b2_ref, o_ref,
                   w1f, w2f, w1b, w2b, sem_w):
    i = pl.program_id(0)
    e = be_ref[i]
    slot = par_ref[i]

    def weight_copies(expert, s):
        return (pltpu.make_async_copy(w1_hbm.at[layer, expert], w1f.at[s], sem_w.at[s]),
                pltpu.make_async_copy(w2_hbm.at[layer, expert], w2f.at[s], sem_w.at[s]))

    @pl.when(i < nu_ref[0])
    def _():
        prev_e = be_ref[jnp.maximum(i - 1, 0)]

        @pl.when(jnp.logical_or(i == 0, e != prev_e))
        def _():
            @pl.when(i == 0)
            def _():
                for cp in weight_copies(e, slot):
                    cp.start()

            for cp in weight_copies(e, slot):
                cp.wait()
            w1b[...] = w1f[slot].astype(BF16)
            w2b[...] = w2f[slot].astype(BF16)
            nxt = nx_ref[i]

            @pl.when(nxt >= 0)
            def _():
                for cp in weight_copies(nxt, 1 - slot):
                    cp.start()

        xs = jnp.concatenate(
            [_load_row_tile_chunk(x_ref, 0, MOE_M, c) for c in range(ROW_TILE)], axis=1).astype(BF16)
        u = jnp.dot(xs, w1b[...], preferred_element_type=F32) + b1_ref[0, 0]
        glu = jnp.minimum(u[:, :D_EXPERT], SWIGLU_LIMIT)
        lin = jnp.clip(u[:, D_EXPERT:], -SWIGLU_LIMIT, SWIGLU_LIMIT)
        act = glu * _sigmoid(SWIGLU_ALPHA * glu) * (lin + 1.0)
        y = jnp.dot(act.astype(BF16), w2b[...], preferred_element_type=F32) + b2_ref[0, 0]
        _store_row_tiles(o_ref, 0, MOE_M, y)


def _experts(layer, tables, xs, w1, b1, w2, b2):
    bmap = lambda i, be, br, par, nx, nu: (layer, be[i], 0, 0)
    rmap = lambda i, be, br, par, nx, nu: (br[i], 0)
    grid_spec = pltpu.PrefetchScalarGridSpec(
        num_scalar_prefetch=5,
        grid=(N_BLOCKS,),
        in_specs=[
            pl.BlockSpec((MOE_M * ROW_TILE, LANES), rmap),
            pl.BlockSpec(memory_space=pl.ANY),
            pl.BlockSpec((1, 1, 1, 2 * D_EXPERT), bmap),
            pl.BlockSpec(memory_space=pl.ANY),
            pl.BlockSpec((1, 1, 1, D_MODEL), bmap),
        ],
        out_specs=pl.BlockSpec((MOE_M * ROW_TILE, LANES), rmap),
        scratch_shapes=[
            pltpu.VMEM((2, D_MODEL, 2 * D_EXPERT), F32),
            pltpu.VMEM((2, D_EXPERT, D_MODEL), F32),
            pltpu.VMEM((D_MODEL, 2 * D_EXPERT), BF16),
            pltpu.VMEM((D_EXPERT, D_MODEL), BF16),
            pltpu.SemaphoreType.DMA((2,)),
        ],
    )
    return pl.pallas_call(
        functools.partial(_expert_kernel, layer),
        grid_spec=grid_spec,
        out_shape=jax.ShapeDtypeStruct((N_EXPERTS * EXPERT_CAP * ROW_TILE, LANES), F32),
        compiler_params=pltpu.CompilerParams(
            dimension_semantics=("arbitrary",), vmem_limit_bytes=VMEM_LIMIT),
        name="experts",
    )(*tables, xs, w1, b1.reshape(DEPTH, N_EXPERTS, 1, 2 * D_EXPERT), w2,
      b2.reshape(DEPTH, N_EXPERTS, 1, D_MODEL))


def _block_tables(counts):
    ids = jnp.arange(N_EXPERTS, dtype=I32)
    nblk = (counts + MOE_M - 1) // MOE_M
    blk_end = jnp.cumsum(nblk)
    blk_start = blk_end - nblk
    n_used = blk_end[-1]
    step = jnp.minimum(jnp.arange(N_BLOCKS, dtype=I32), n_used - 1)
    block_exp = jnp.minimum(
        jnp.sum((step[:, None] >= blk_end[None, :]).astype(I32), axis=1), N_EXPERTS - 1)
    block_row = block_exp * (EXPERT_CAP // MOE_M) + (step - blk_start[block_exp])
    has_rows = nblk > 0
    slot = (jnp.cumsum(has_rows.astype(I32)) - 1) % 2
    later = jnp.where(has_rows[None, :] & (ids[None, :] > ids[:, None]), ids[None, :], N_EXPERTS)
    nxt = jnp.min(later, axis=1)
    nxt = jnp.where(nxt == N_EXPERTS, -1, nxt)
    return (block_exp.astype(I32), block_row.astype(I32), slot[block_exp].astype(I32),
            nxt[block_exp].astype(I32), n_used.astype(I32).reshape(1))


def _gather_combine(tile, idx0_ref, idxn_ref, ys_hbm, x1_ref, g2_ref, gate_ref, o_ref, bufs, sem,
                    then=None):
    i = pl.program_id(0)
    n = pl.num_programs(0)
    rows = tile * TOP_K

    def wait_gather(s):
        pltpu.make_async_copy(ys_hbm.at[pl.ds(0, rows * ROW_TILE), :], bufs[s], sem.at[s]).wait()

    @pl.when(i == 0)
    def _():
        def body(r, c):
            for u in range(DMA_UNROLL):
                j = r * DMA_UNROLL + u
                _tile_copy(ys_hbm, idx0_ref[0, 0, j], bufs[0], j, sem.at[0]).start(priority=u % 2)
            return c

        lax.fori_loop(0, rows // DMA_UNROLL, body, 0)

    def step(cur, nxt):
        wait_gather(cur)
        gates = gate_ref[...]
        g2 = g2_ref[0]
        for c in range(ROW_TILE):
            cs = slice(c * LANES, (c + 1) * LANES)
            acc = jnp.zeros((tile, LANES), F32)
            for kk in range(TOP_K):
                acc = acc + gates[:, kk:kk + 1] * _load_row_tile_chunk(
                    bufs[cur], kk * tile * ROW_TILE, tile, c)
            o_ref[:, cs] = x1_ref[:, cs] + g2[:, cs] * acc
        if then is not None:
            then()
        for j in range(rows):
            _tile_copy(ys_hbm, idxn_ref[0, 0, j], bufs[nxt], j, sem.at[nxt]).start(priority=j % 2)

        @pl.when(i == n - 1)
        def _():
            wait_gather(nxt)

    @pl.when(i % 2 == 0)
    def _():
        step(0, 1)

    @pl.when(i % 2 == 1)
    def _():
        step(1, 0)


def _combine_kernel(idx0_ref, idxn_ref, ys_hbm, x1_ref, g2_ref, gate_ref, o_ref, buf0, buf1, sem):
    _gather_combine(TMC, idx0_ref, idxn_ref, ys_hbm, x1_ref, g2_ref, gate_ref, o_ref, (buf0, buf1), sem)


def _combine_inproj_kernel(idx0_ref, idxn_ref, ys_hbm, x1_ref, g2_ref, gate_ref,
                           sh_ref, sc_ref, g_ref, w_ref, seg_ref, qg_ref, kg_ref,
                           xn_ref, ug_ref, q_ref, k_ref, v_ref, buf0, buf1, sem):
    def inproj():
        _inproj_body(xn_ref[...], sh_ref, sc_ref, g_ref, w_ref, seg_ref, qg_ref, kg_ref,
                     ug_ref, q_ref, k_ref, v_ref)

    _gather_combine(TM, idx0_ref, idxn_ref, ys_hbm, x1_ref, g2_ref, gate_ref, xn_ref, (buf0, buf1), sem,
                    then=inproj)


def _choice_major(dest, tile):
    return dest.reshape(TOP_K, TOKENS // tile, tile).transpose(1, 0, 2).reshape(
        TOKENS // tile, 1, TOP_K * tile)


def _combine_inproj(dest, ys, x1, mod3_prev, gates, mod3, norm_g, w_in_bf, seg, qg, kg):
    n_tiles = TOKENS // TM
    rows = TM * TOP_K
    row = lambda i: (i, 0)
    const = lambda i: (0, 0)
    modspec = lambda j: pl.BlockSpec((1, 1, D_MODEL), lambda i: (i // TILES_PER_SEQ, 0, j))
    half_sd = jax.ShapeDtypeStruct((TOKENS, D_CONV), BF16)
    return pl.pallas_call(
        _combine_inproj_kernel,
        grid=(n_tiles,),
        in_specs=[
            pl.BlockSpec((1, 1, rows), lambda i: (0, 0, 0), memory_space=pltpu.SMEM),
            pl.BlockSpec((1, 1, rows), lambda i: (jnp.minimum(i + 1, n_tiles - 1), 0, 0),
                         memory_space=pltpu.SMEM),
            pl.BlockSpec(memory_space=pl.ANY),
            pl.BlockSpec((TM, D_MODEL), row),
            modspec(5),
            pl.BlockSpec((TM, TOP_K), row),
            modspec(0),
            modspec(1),
            pl.BlockSpec((1, D_MODEL), const),
            pl.BlockSpec((D_MODEL, D_IN), const),
            pl.BlockSpec((D_ATTN, D_ATTN), const),
            pl.BlockSpec((1, D_ATTN), const),
            pl.BlockSpec((1, D_ATTN), const),
        ],
        out_specs=[pl.BlockSpec((TM, D_MODEL), row)] + [pl.BlockSpec((TM, D_CONV), row)] * 4,
        out_shape=[jax.ShapeDtypeStruct((TOKENS, D_MODEL), F32)] + [half_sd] * 4,
        scratch_shapes=[
            pltpu.VMEM((rows * ROW_TILE, LANES), F32),
            pltpu.VMEM((rows * ROW_TILE, LANES), F32),
            pltpu.SemaphoreType.DMA((2,)),
        ],
        compiler_params=pltpu.CompilerParams(
            dimension_semantics=("arbitrary",), vmem_limit_bytes=VMEM_LIMIT),
        name="combine_inproj",
    )(_choice_major(dest, TM), _choice_major(dest, TM), ys, x1, mod3_prev, gates,
      mod3, mod3, norm_g, w_in_bf, seg, qg, kg)


def _combine(dest3, ys, x1, mod3, gates):
    n_tiles = TOKENS // TMC
    tiles_per_seq = SEQ // TMC
    rows = TMC * TOP_K
    return pl.pallas_call(
        _combine_kernel,
        grid=(n_tiles,),
        in_specs=[
            pl.BlockSpec((1, 1, rows), lambda i: (0, 0, 0), memory_space=pltpu.SMEM),
            pl.BlockSpec((1, 1, rows), lambda i: (jnp.minimum(i + 1, n_tiles - 1), 0, 0),
                         memory_space=pltpu.SMEM),
            pl.BlockSpec(memory_space=pl.ANY),
            pl.BlockSpec((TMC, D_MODEL), lambda i: (i, 0)),
            pl.BlockSpec((1, 1, D_MODEL), lambda i: (i // tiles_per_seq, 0, 5)),
            pl.BlockSpec((TMC, TOP_K), lambda i: (i, 0)),
        ],
        out_specs=pl.BlockSpec((TMC, D_MODEL), lambda i: (i, 0)),
        out_shape=jax.ShapeDtypeStruct((TOKENS, D_MODEL), F32),
        scratch_shapes=[
            pltpu.VMEM((rows * ROW_TILE, LANES), F32),
            pltpu.VMEM((rows * ROW_TILE, LANES), F32),
            pltpu.SemaphoreType.DMA((2,)),
        ],
        compiler_params=pltpu.CompilerParams(
            dimension_semantics=("arbitrary",), vmem_limit_bytes=VMEM_LIMIT),
        name="moe_combine",
    )(dest3, dest3, ys, x1, mod3, gates)


_I = np.arange(Q_PAIR)[:, None]
_K = np.arange(KEY_WIN)[None, :]
_BAND_OK = (_K // CHUNK >= _I // CHUNK) & (_K // CHUNK <= _I // CHUNK + LEFT_CHUNKS)
_SEG = np.kron(np.eye(N_HEADS), np.full((HEAD_DIM, HEAD_DIM), 1.0 / HEAD_DIM)).astype(np.float32)
_TRI = (np.arange(TM)[:, None] < np.arange(TM)[None, :]).astype(np.float32)


def _bias_tables(rel_bias):
    n_far = KEY_WIN - 1 + Q_PAIR - 2 * REL_CLIP
    ext = jnp.concatenate(
        [rel_bias[..., 1:], jnp.repeat(rel_bias[..., 2 * REL_CLIP:], n_far, axis=-1)], axis=-1)
    rev = ext[..., ::-1]
    period = KEY_WIN + Q_PAIR
    ring = jnp.concatenate(
        [rev[..., Q_PAIR - 1:], jnp.zeros(rev.shape[:-1] + (1,), rev.dtype), rev[..., :Q_PAIR - 1]],
        axis=-1)
    lead = rel_bias.shape[:-1]
    skew = jnp.tile(ring, Q_PAIR)[..., :Q_PAIR * (period - 1)].reshape(lead + (Q_PAIR, period - 1))
    table = jnp.where(_BAND_OK, skew[..., :KEY_WIN], NEG_INF).astype(F32)
    return table.reshape(DEPTH, N_HEADS // 2, 2, Q_PAIR, KEY_WIN).transpose(0, 1, 4, 2, 3).reshape(
        DEPTH, N_HEADS // 2, KEY_WIN, 2 * Q_PAIR)


def kernel(x, c, ada_w, ada_b, norm1_g, w_in, conv_w, conv_b, conv_ln_g, conv_ln_b, q_norm_g,
           k_norm_g, rel_bias, w_out, norm2_g, router_w, router_b, exp_w1, exp_b1, exp_w2, exp_b2):
    xf = x.reshape(TOKENS, D_MODEL)
    mod = _ada_mod(c, ada_w, ada_b)
    seg = jnp.asarray(_SEG, BF16)
    tri = jnp.asarray(_TRI, BF16)
    pad_e = ((0, 0), (0, LANES - N_EXPERTS))
    bias_tables = _bias_tables(rel_bias)
    moe = None
    for l in range(DEPTH):
        mod3 = mod[l].reshape(BATCH, 1, 6 * D_MODEL)
        qg = (jnp.tile(q_norm_g[l], N_HEADS) * (HEAD_DIM ** -0.5)).reshape(1, D_ATTN)
        kg = jnp.tile(k_norm_g[l], N_HEADS).reshape(1, D_ATTN)
        inproj_params = (mod3, norm1_g[l].reshape(1, D_MODEL), w_in[l].astype(BF16), seg, qg, kg)
        if moe is None:
            ug, q, k, v = _inproj(xf, *inproj_params)
        else:
            xf, ug, q, k, v = _combine_inproj(*moe, *inproj_params)
        y_conv = _conv(ug, conv_w[l], conv_b[l].reshape(1, D_CONV),
                       conv_ln_g[l].reshape(1, D_CONV), conv_ln_b[l].reshape(1, D_CONV))
        y_attn = _attention(q, k, v, bias_tables[l])
        w_out_bf = w_out[l].astype(BF16)
        rw_hi = router_w[l].astype(BF16)
        rw_lo = (router_w[l] - rw_hi.astype(F32)).astype(BF16)
        x1, dest, gates, cnt, xs = _outproj(
            y_conv, y_attn, xf, mod3, w_out_bf[:D_CONV], w_out_bf[D_CONV:],
            norm2_g[l].reshape(1, D_MODEL), jnp.pad(rw_hi, pad_e), jnp.pad(rw_lo, pad_e),
            jnp.pad(router_b[l].reshape(1, N_EXPERTS), pad_e), tri)
        ys = _experts(l, _block_tables(cnt[:, 0].astype(I32)), xs, exp_w1, exp_b1, exp_w2, exp_b2)
        moe = (dest, ys, x1, mod3, gates.T)
    dest, ys, x1, mod3, gates_t = moe
    xf = _combine(_choice_major(dest, TMC), ys, x1, mod3, gates_t)
    return xf.reshape(BATCH, SEQ, D_MODEL)
```

```python
import functools

import jax
import jax.numpy as jnp
import numpy as np
from jax import lax
from jax.experimental import pallas as pl
from jax.experimental.pallas import tpu as pltpu

D_MODEL = 1024
BATCH = 8
SEQ = 2048
DEPTH = 4
TOKENS = BATCH * SEQ

CHUNK = 64
D_CONV = D_MODEL // 2
D_ATTN = D_MODEL - D_CONV
N_HEADS = 8
HEAD_DIM = D_ATTN // N_HEADS
CONV_KERNEL = 31
LEFT_CHUNKS = 8
REL_CLIP = 128
N_EXPERTS = 32
TOP_K = 4
D_EXPERT = D_MODEL
SWIGLU_ALPHA = 1.702
SWIGLU_LIMIT = 7.0
EPS = 1e-6
D_IN = 2 * D_CONV + 3 * D_ATTN

F32 = jnp.float32
BF16 = jnp.bfloat16
I32 = jnp.int32
NEG_INF = float("-inf")

LANES = 128
SUBLANES = 8
TM = 512
TILES_PER_SEQ = SEQ // TM
Q_PAIR = 2 * CHUNK
KEY_WIN = (LEFT_CHUNKS + 2) * CHUNK
PAIRS_PER_TILE = TM // Q_PAIR
ATTN_SKEW = 3
HALO = 32
CONV_ROWS = 32
CONV_CHUNKS_PER_ITER = 4
MOE_M = 512
ZERO_ROWS = MOE_M // 2
N_BLOCKS = TOKENS * TOP_K // MOE_M + N_EXPERTS
EXPERT_CAP = TOKENS
ROW_TILE = D_MODEL // LANES
TMC = 256
DMA_UNROLL = 8

VMEM_LIMIT = 56 * 1024 * 1024


def _sigmoid(x):
    return 1.0 / (1.0 + jnp.exp(-x))


def _ada_kernel(c_ref, w_ref, b_ref, o_ref):
    c = c_ref[...]
    ca = (c * _sigmoid(c)).astype(BF16)
    o_ref[0] = jnp.dot(ca, w_ref[0].astype(BF16), preferred_element_type=F32) + b_ref[0]


def _ada_mod(c, ada_w, ada_b):
    n_tile = 1536
    return pl.pallas_call(
        _ada_kernel,
        grid=(DEPTH, 6 * D_MODEL // n_tile),
        in_specs=[
            pl.BlockSpec((BATCH, D_MODEL), lambda l, n: (0, 0)),
            pl.BlockSpec((1, D_MODEL, n_tile), lambda l, n: (l, 0, n)),
            pl.BlockSpec((1, 1, n_tile), lambda l, n: (l, 0, n)),
        ],
        out_specs=pl.BlockSpec((1, BATCH, n_tile), lambda l, n: (l, 0, n)),
        out_shape=jax.ShapeDtypeStruct((DEPTH, BATCH, 6 * D_MODEL), F32),
        compiler_params=pltpu.CompilerParams(
            dimension_semantics=("arbitrary", "arbitrary"), vmem_limit_bytes=VMEM_LIMIT),
        name="ada_mod",
    )(c, ada_w, ada_b.reshape(DEPTH, 1, 6 * D_MODEL))


def _modulated_norm(x, g, sc, sh):
    ms = jnp.mean(x * x, axis=-1, keepdims=True)
    return x * lax.rsqrt(ms + EPS) * g * (1.0 + sc) + sh


def _inproj_kernel(x_ref, *refs):
    _inproj_body(x_ref[...], *refs)


def _inproj_body(x, sh_ref, sc_ref, g_ref, w_ref, seg_ref, qg_ref, kg_ref, ug_ref, q_ref, k_ref, v_ref):
    h = _modulated_norm(x, g_ref[...], sc_ref[0], sh_ref[0])
    u = jnp.dot(h.astype(BF16), w_ref[...], preferred_element_type=F32)
    a = u[:, :D_CONV]
    gt = u[:, D_CONV:2 * D_CONV]
    ug_ref[...] = (a * _sigmoid(gt)).astype(BF16)
    o = 2 * D_CONV
    q = u[:, o:o + D_ATTN]
    k = u[:, o + D_ATTN:o + 2 * D_ATTN]
    v = u[:, o + 2 * D_ATTN:]
    qms = jnp.dot((q * q).astype(BF16), seg_ref[...], preferred_element_type=F32)
    kms = jnp.dot((k * k).astype(BF16), seg_ref[...], preferred_element_type=F32)
    q_ref[...] = (q * lax.rsqrt(qms + EPS) * qg_ref[...]).astype(BF16)
    k_ref[...] = (k * lax.rsqrt(kms + EPS) * kg_ref[...]).astype(BF16)
    v_ref[...] = v.astype(BF16)


def _inproj(x, mod3, norm_g, w_in_bf, seg, qg, kg):
    row = lambda i: (i, 0)
    const = lambda i: (0, 0)
    out_sd = jax.ShapeDtypeStruct((TOKENS, D_CONV), BF16)
    return pl.pallas_call(
        _inproj_kernel,
        grid=(TOKENS // TM,),
        in_specs=[
            pl.BlockSpec((TM, D_MODEL), row),
            pl.BlockSpec((1, 1, D_MODEL), lambda i: (i // TILES_PER_SEQ, 0, 0)),
            pl.BlockSpec((1, 1, D_MODEL), lambda i: (i // TILES_PER_SEQ, 0, 1)),
            pl.BlockSpec((1, D_MODEL), const),
            pl.BlockSpec((D_MODEL, D_IN), const),
            pl.BlockSpec((D_ATTN, D_ATTN), const),
            pl.BlockSpec((1, D_ATTN), const),
            pl.BlockSpec((1, D_ATTN), const),
        ],
        out_specs=[pl.BlockSpec((TM, D_CONV), row)] * 4,
        out_shape=[out_sd] * 4,
        compiler_params=pltpu.CompilerParams(
            dimension_semantics=("arbitrary",), vmem_limit_bytes=VMEM_LIMIT),
        name="inproj",
    )(x, mod3, mod3, norm_g, w_in_bf, seg, qg, kg)


def _conv_kernel(cur_ref, halo_ref, w_ref, b_ref, lg_ref, lb_ref, o_ref, xe_ref):
    i = pl.program_id(0)
    seq_start = (i % TILES_PER_SEQ) == 0
    halo = halo_ref[...].astype(F32)
    xe_ref[0, 0:HALO, :] = jnp.where(seq_start, 0.0, halo)
    xe_ref[0, HALO:, :] = cur_ref[...].astype(F32)
    ext = HALO + TM
    for s in range(1, SUBLANES):
        xe_ref[s, 0:ext - SUBLANES, :] = xe_ref[0, s:ext - SUBLANES + s, :]
    n_lane = D_CONV // LANES
    first_tap = HALO - (CONV_KERNEL - 1)

    lane_slices = [slice(c * LANES, (c + 1) * LANES) for c in range(n_lane)]

    def taps(r0):
        accs = [jnp.zeros((CONV_ROWS, LANES), F32) + b_ref[:, cs] for cs in lane_slices]
        for j in range(CONV_KERNEL):
            shift = (first_tap + j) % SUBLANES
            aligned = first_tap + j - shift
            for c, cs in enumerate(lane_slices):
                accs[c] = accs[c] + w_ref[j:j + 1, cs] * xe_ref[shift, pl.ds(r0 + aligned, CONV_ROWS), cs]
        return accs

    def norm_act(r0, accs):
        tot = accs[0]
        for c in range(1, n_lane):
            tot = tot + accs[c]
        mu = tot.sum(axis=-1, keepdims=True) * (1.0 / D_CONV)
        cen = [a - mu for a in accs]
        sq = cen[0] * cen[0]
        for c in range(1, n_lane):
            sq = sq + cen[c] * cen[c]
        inv = lax.rsqrt(sq.sum(axis=-1, keepdims=True) * (1.0 / D_CONV) + EPS)
        for c, cs in enumerate(lane_slices):
            y = cen[c] * inv * lg_ref[:, cs] + lb_ref[:, cs]
            o_ref[pl.ds(r0, CONV_ROWS), cs] = (y * _sigmoid(y)).astype(BF16)

    def body(rc, carry):
        pending = None
        for sub in range(CONV_CHUNKS_PER_ITER):
            r0 = pl.multiple_of((rc * CONV_CHUNKS_PER_ITER + sub) * CONV_ROWS, CONV_ROWS)
            accs = taps(r0)
            if pending is not None:
                norm_act(*pending)
            pending = (r0, accs)
        norm_act(*pending)
        return carry

    lax.fori_loop(0, TM // (CONV_ROWS * CONV_CHUNKS_PER_ITER), body, 0)


def _conv(ug, conv_w, conv_b, ln_g, ln_b):
    const = lambda i: (0, 0)
    halo_per_tile = TM // HALO
    return pl.pallas_call(
        _conv_kernel,
        grid=(TOKENS // TM,),
        in_specs=[
            pl.BlockSpec((TM, D_CONV), lambda i: (i, 0)),
            pl.BlockSpec((HALO, D_CONV), lambda i: (jnp.maximum(i * halo_per_tile - 1, 0), 0)),
            pl.BlockSpec((CONV_KERNEL, D_CONV), const),
            pl.BlockSpec((1, D_CONV), const),
            pl.BlockSpec((1, D_CONV), const),
            pl.BlockSpec((1, D_CONV), const),
        ],
        out_specs=pl.BlockSpec((TM, D_CONV), lambda i: (i, 0)),
        out_shape=jax.ShapeDtypeStruct((TOKENS, D_CONV), BF16),
        scratch_shapes=[pltpu.VMEM((SUBLANES, HALO + TM, D_CONV), F32)],
        compiler_params=pltpu.CompilerParams(
            dimension_semantics=("arbitrary",), vmem_limit_bytes=VMEM_LIMIT),
        name="conv_mixer",
    )(ug, ug, conv_w, conv_b, ln_g, ln_b)


def _attn_kernel(q_ref, kp_ref, kc_ref, vp_ref, vc_ref, bias_ref, o_ref, kw_ref, vt_ref):
    i = pl.program_id(0)
    seq_start = (i % TILES_PER_SEQ) == 0
    kw_ref[0:TM, :] = kp_ref[...]
    kw_ref[TM:, :] = kc_ref[...]
    vt_ref[:, 0:TM] = vp_ref[...].T
    vt_ref[:, TM:] = vc_ref[...].T
    pair = 2 * HEAD_DIM
    low = lax.broadcasted_iota(jnp.int32, (Q_PAIR, pair), 1) < HEAD_DIM

    def probs(cp, hp, k_lo):
        r0 = cp * Q_PAIR
        ps = slice(hp * pair, (hp + 1) * pair)
        q2 = q_ref[r0:r0 + Q_PAIR, ps]
        zero = jnp.zeros_like(q2)
        qm = jnp.concatenate([jnp.where(low, q2, zero), jnp.where(low, zero, q2)], axis=0)
        k2 = kw_ref[r0 + k_lo:r0 + KEY_WIN, ps]
        st = lax.dot_general(k2, qm, (((1,), (1,)), ((), ())), preferred_element_type=F32)
        st = st + bias_ref[hp, k_lo:, :]
        m = jnp.max(st, axis=0, keepdims=True)
        e = jnp.exp(st - m)
        return e.astype(BF16), 1.0 / jnp.sum(e, axis=0, keepdims=True)

    def weighted_values(cp, hp, k_lo, e, inv_l):
        r0 = cp * Q_PAIR
        ps = slice(hp * pair, (hp + 1) * pair)
        ot = jnp.dot(vt_ref[ps, r0 + k_lo:r0 + KEY_WIN], e,
                     preferred_element_type=F32) * inv_l
        o2t = jnp.concatenate([ot[:HEAD_DIM, :Q_PAIR], ot[HEAD_DIM:, Q_PAIR:]], axis=0)
        o_ref[r0:r0 + Q_PAIR, ps] = o2t.T.astype(BF16)

    def all_blocks(k_lo_of):
        pending = []
        for cp in range(PAIRS_PER_TILE):
            for hp in range(N_HEADS // 2):
                block = (cp, hp, k_lo_of(cp))
                pending.append(block + probs(*block))
                if len(pending) > ATTN_SKEW:
                    weighted_values(*pending.pop(0))
        for item in pending:
            weighted_values(*item)

    @pl.when(seq_start)
    def _():
        all_blocks(lambda cp: TM - cp * Q_PAIR)

    @pl.when(jnp.logical_not(seq_start))
    def _():
        all_blocks(lambda cp: 0)


def _attention(q, k, v, bias2):
    cur = lambda i: (i, 0)
    prev = lambda i: (jnp.maximum(i - 1, 0), 0)
    blk = (TM, D_ATTN)
    return pl.pallas_call(
        _attn_kernel,
        grid=(TOKENS // TM,),
        in_specs=[
            pl.BlockSpec(blk, cur),
            pl.BlockSpec(blk, prev), pl.BlockSpec(blk, cur),
            pl.BlockSpec(blk, prev), pl.BlockSpec(blk, cur),
            pl.BlockSpec((N_HEADS // 2, KEY_WIN, 2 * Q_PAIR), lambda i: (0, 0, 0)),
        ],
        out_specs=pl.BlockSpec(blk, cur),
        out_shape=jax.ShapeDtypeStruct((TOKENS, D_ATTN), BF16),
        scratch_shapes=[pltpu.VMEM((2 * TM, D_ATTN), BF16), pltpu.VMEM((D_ATTN, 2 * TM), BF16)],
        compiler_params=pltpu.CompilerParams(
            dimension_semantics=("arbitrary",), vmem_limit_bytes=VMEM_LIMIT),
        name="chunk_attention",
    )(q, k, k, v, v, bias2)


def _store_row_tiles(ref, base, rows, value):
    for c in range(ROW_TILE):
        ref[pl.ds(base + c, rows, stride=ROW_TILE), :] = value[:, c * LANES:(c + 1) * LANES]


def _load_row_tile_chunk(ref, base, rows, c):
    return ref[pl.ds(base + c, rows, stride=ROW_TILE), :]


def _tile_copy(src_ref, src_row, dst_ref, dst_row, sem):
    return pltpu.make_async_copy(
        src_ref.at[pl.ds(src_row * ROW_TILE, ROW_TILE), :],
        dst_ref.at[pl.ds(dst_row * ROW_TILE, ROW_TILE), :], sem)


def _outproj_kernel(yc_ref, ya_ref, x_ref, g1_ref, wt_ref, wb_ref, sh_ref, sc_ref, g_ref,
                    rwh_ref, rwl_ref, rb_ref, tri_ref,
                    x1_ref, dest_ref, gate_ref, cnt_ref, xs_hbm,
                    hbuf, dvm, dsm, cvm, csm, zbuf, carry, sem_idx, sem_push, sem_zero):
    i = pl.program_id(0)
    n = pl.num_programs(0)
    slot = i % 2

    @pl.when(i == 0)
    def _():
        carry[...] = jnp.zeros_like(carry)

    mixed = (jnp.dot(yc_ref[...], wt_ref[...], preferred_element_type=F32)
             + jnp.dot(ya_ref[...], wb_ref[...], preferred_element_type=F32))
    x1 = x_ref[...] + g1_ref[0] * mixed
    x1_ref[...] = x1
    h2 = _modulated_norm(x1, g_ref[...], sc_ref[0], sh_ref[0])

    hi = h2.astype(BF16)
    lo = (h2 - hi.astype(F32)).astype(BF16)
    logits = (jnp.dot(hi, rwh_ref[...], preferred_element_type=F32)
              + jnp.dot(hi, rwl_ref[...], preferred_element_type=F32)
              + jnp.dot(lo, rwh_ref[...], preferred_element_type=F32)) + rb_ref[...]
    lt = logits.T[:N_EXPERTS, :]

    eid = lax.broadcasted_iota(I32, (N_EXPERTS, TM), 0)
    picks, onehots, exps = [], [], []
    top0 = None
    for kk in range(TOP_K):
        m = jnp.max(lt, axis=0, keepdims=True)
        am = jnp.min(jnp.where(lt == m, eid, N_EXPERTS), axis=0, keepdims=True)
        oh = eid == am
        lt = jnp.where(oh, NEG_INF, lt)
        if kk == 0:
            top0 = m
        picks.append(am)
        onehots.append(oh)
        exps.append(jnp.exp(m - top0))
    denom = exps[0] + exps[1] + exps[2] + exps[3]
    gate_ref[...] = jnp.concatenate(exps, axis=0) * (1.0 / denom)

    oh_all = (onehots[0].astype(F32) + onehots[1].astype(F32)
              + onehots[2].astype(F32) + onehots[3].astype(F32))
    before = jnp.dot(oh_all.astype(BF16), tri_ref[...], preferred_element_type=F32) + carry[:, 0:1]
    dests = []
    for kk in range(TOP_K):
        rank = jnp.sum(jnp.where(onehots[kk], before, 0.0), axis=0, keepdims=True)
        dests.append(picks[kk] * EXPERT_CAP + rank.astype(I32))
    dest = jnp.concatenate(dests, axis=0)
    dest_ref[...] = dest
    carry[...] = carry[...] + jnp.sum(oh_all, axis=1, keepdims=True)
    cnt_ref[...] = carry[...]

    dvm[...] = jnp.concatenate([dest, jnp.zeros((SUBLANES - TOP_K, TM), I32)], axis=0)
    idx_copy = pltpu.make_async_copy(dvm, dsm, sem_idx)
    idx_copy.start()

    hb = pl.multiple_of(slot * (TM * ROW_TILE), TM * ROW_TILE)
    _store_row_tiles(hbuf, hb, TM, h2)

    def wait_push(s):
        base = pl.multiple_of(s * (TM * ROW_TILE), TM * ROW_TILE)
        for _ in range(TOP_K):
            pltpu.make_async_copy(hbuf.at[pl.ds(base, TM * ROW_TILE), :],
                                  xs_hbm.at[pl.ds(0, TM * ROW_TILE), :], sem_push.at[s]).wait()

    @pl.when(i > 0)
    def _():
        wait_push(1 - slot)

    idx_copy.wait()
    per_iter = DMA_UNROLL // TOP_K

    def push_body(r, c):
        for u in range(DMA_UNROLL):
            t = r * per_iter + u // TOP_K
            d = dsm[u % TOP_K, t]
            _tile_copy(hbuf, slot * TM + t, xs_hbm, d, sem_push.at[slot]).start(priority=u % 2)
        return c

    lax.fori_loop(0, TM // per_iter, push_body, 0)

    @pl.when(i == n - 1)
    def _():
        wait_push(slot)
        zbuf[...] = jnp.zeros_like(zbuf)
        cvm[...] = carry[...].astype(I32)
        cnt_copy = pltpu.make_async_copy(cvm, csm, sem_idx)
        cnt_copy.start()
        cnt_copy.wait()

        def per_expert(e, c, wait):
            cnt = csm[e, 0]
            pad = (MOE_M - cnt % MOE_M) % MOE_M
            for b in range(ZERO_ROWS.bit_length()):
                size = (1 << b) * ROW_TILE
                start = (e * EXPERT_CAP + cnt + (pad & ((1 << b) - 1))) * ROW_TILE

                @pl.when(((pad >> b) & 1) == 1)
                def _():
                    cp = pltpu.make_async_copy(zbuf.at[pl.ds(0, size), :],
                                               xs_hbm.at[pl.ds(start, size), :], sem_zero)
                    if wait:
                        cp.wait()
                    else:
                        cp.start()
            return c

        lax.fori_loop(0, N_EXPERTS, lambda e, c: per_expert(e, c, False), 0)
        lax.fori_loop(0, N_EXPERTS, lambda e, c: per_expert(e, c, True), 0)


def _outproj(yc, ya, x, mod3, w_top, w_bot, norm_g, rw_hi, rw_lo, rb, tri):
    row = lambda i: (i, 0)
    col = lambda i: (0, i)
    const = lambda i: (0, 0)
    modspec = lambda j: pl.BlockSpec((1, 1, D_MODEL), lambda i: (i // TILES_PER_SEQ, 0, j))
    return pl.pallas_call(
        _outproj_kernel,
        grid=(TOKENS // TM,),
        in_specs=[
            pl.BlockSpec((TM, D_CONV), row),
            pl.BlockSpec((TM, D_ATTN), row),
            pl.BlockSpec((TM, D_MODEL), row),
            modspec(2),
            pl.BlockSpec((D_CONV, D_MODEL), const),
            pl.BlockSpec((D_ATTN, D_MODEL), const),
            modspec(3),
            modspec(4),
            pl.BlockSpec((1, D_MODEL), const),
            pl.BlockSpec((D_MODEL, LANES), const),
            pl.BlockSpec((D_MODEL, LANES), const),
            pl.BlockSpec((1, LANES), const),
            pl.BlockSpec((TM, TM), const),
        ],
        out_specs=[
            pl.BlockSpec((TM, D_MODEL), row),
            pl.BlockSpec((TOP_K, TM), col),
            pl.BlockSpec((TOP_K, TM), col),
            pl.BlockSpec((N_EXPERTS, LANES), const),
            pl.BlockSpec(memory_space=pl.ANY),
        ],
        out_shape=[
            jax.ShapeDtypeStruct((TOKENS, D_MODEL), F32),
            jax.ShapeDtypeStruct((TOP_K, TOKENS), I32),
            jax.ShapeDtypeStruct((TOP_K, TOKENS), F32),
            jax.ShapeDtypeStruct((N_EXPERTS, LANES), F32),
            jax.ShapeDtypeStruct((N_EXPERTS * EXPERT_CAP * ROW_TILE, LANES), F32),
        ],
        scratch_shapes=[
            pltpu.VMEM((2 * TM * ROW_TILE, LANES), F32),
            pltpu.VMEM((SUBLANES, TM), I32),
            pltpu.SMEM((SUBLANES, TM), I32),
            pltpu.VMEM((N_EXPERTS, LANES), I32),
            pltpu.SMEM((N_EXPERTS, LANES), I32),
            pltpu.VMEM((ZERO_ROWS * ROW_TILE, LANES), F32),
            pltpu.VMEM((N_EXPERTS, LANES), F32),
            pltpu.SemaphoreType.DMA(()),
            pltpu.SemaphoreType.DMA((2,)),
            pltpu.SemaphoreType.DMA(()),
        ],
        compiler_params=pltpu.CompilerParams(
            dimension_semantics=("arbitrary",), vmem_limit_bytes=VMEM_LIMIT),
        name="outproj_router",
    )(yc, ya, x, mod3, w_top, w_bot, mod3, mod3, norm_g, rw_hi, rw_lo, rb, tri)


def _expert_kernel(layer, be_ref, br_ref, par_ref, nx_ref, nu_ref,
                   x_ref, w1_hbm, b1_ref, w2_hbm, b2_ref, o_ref,
                   w1f, w2f, w1b, w2b, sem_w):
    i = pl.program_id(0)
    e = be_ref[i]
    slot = par_ref[i]

    def weight_copies(expert, s):
        return (pltpu.make_async_copy(w1_hbm.at[layer, expert], w1f.at[s], sem_w.at[s]),
                pltpu.make_async_copy(w2_hbm.at[layer, expert], w2f.at[s], sem_w.at[s]))

    @pl.when(i < nu_ref[0])
    def _():
        prev_e = be_ref[jnp.maximum(i - 1, 0)]

        @pl.when(jnp.logical_or(i == 0, e != prev_e))
        def _():
            @pl.when(i == 0)
            def _():
                for cp in weight_copies(e, slot):
                    cp.start()

            for cp in weight_copies(e, slot):
                cp.wait()
            w1b[...] = w1f[slot].astype(BF16)
            w2b[...] = w2f[slot].astype(BF16)
            nxt = nx_ref[i]

            @pl.when(nxt >= 0)
            def _():
                for cp in weight_copies(nxt, 1 - slot):
                    cp.start()

        xs = jnp.concatenate(
            [_load_row_tile_chunk(x_ref, 0, MOE_M, c) for c in range(ROW_TILE)], axis=1).astype(BF16)
        u = jnp.dot(xs, w1b[...], preferred_element_type=F32) + b1_ref[0, 0]
        glu = jnp.minimum(u[:, :D_EXPERT], SWIGLU_LIMIT)
        lin = jnp.clip(u[:, D_EXPERT:], -SWIGLU_LIMIT, SWIGLU_LIMIT)
        act = glu * _sigmoid(SWIGLU_ALPHA * glu) * (lin + 1.0)
        y = jnp.dot(act.astype(BF16), w2b[...], preferred_element_type=F32) + b2_ref[0, 0]
        _store_row_tiles(o_ref, 0, MOE_M, y)


def _experts(layer, tables, xs, w1, b1, w2, b2):
    bmap = lambda i, be, br, par, nx, nu: (layer, be[i], 0, 0)
    rmap = lambda i, be, br, par, nx, nu: (br[i], 0)
    grid_spec = pltpu.PrefetchScalarGridSpec(
        num_scalar_prefetch=5,
        grid=(N_BLOCKS,),
        in_specs=[
            pl.BlockSpec((MOE_M * ROW_TILE, LANES), rmap),
            pl.BlockSpec(memory_space=pl.ANY),
            pl.BlockSpec((1, 1, 1, 2 * D_EXPERT), bmap),
            pl.BlockSpec(memory_space=pl.ANY),
            pl.BlockSpec((1, 1, 1, D_MODEL), bmap),
        ],
        out_specs=pl.BlockSpec((MOE_M * ROW_TILE, LANES), rmap),
        scratch_shapes=[
            pltpu.VMEM((2, D_MODEL, 2 * D_EXPERT), F32),
            pltpu.VMEM((2, D_EXPERT, D_MODEL), F32),
            pltpu.VMEM((D_MODEL, 2 * D_EXPERT), BF16),
            pltpu.VMEM((D_EXPERT, D_MODEL), BF16),
            pltpu.SemaphoreType.DMA((2,)),
        ],
    )
    return pl.pallas_call(
        functools.partial(_expert_kernel, layer),
        grid_spec=grid_spec,
        out_shape=jax.ShapeDtypeStruct((N_EXPERTS * EXPERT_CAP * ROW_TILE, LANES), F32),
        compiler_params=pltpu.CompilerParams(
            dimension_semantics=("arbitrary",), vmem_limit_bytes=VMEM_LIMIT),
        name="experts",
    )(*tables, xs, w1, b1.reshape(DEPTH, N_EXPERTS, 1, 2 * D_EXPERT), w2,
      b2.reshape(DEPTH, N_EXPERTS, 1, D_MODEL))


def _block_tables(counts):
    ids = jnp.arange(N_EXPERTS, dtype=I32)
    nblk = (counts + MOE_M - 1) // MOE_M
    blk_end = jnp.cumsum(nblk)
    blk_start = blk_end - nblk
    n_used = blk_end[-1]
    step = jnp.minimum(jnp.arange(N_BLOCKS, dtype=I32), n_used - 1)
    block_exp = jnp.minimum(
        jnp.sum((step[:, None] >= blk_end[None, :]).astype(I32), axis=1), N_EXPERTS - 1)
    is_exp = block_exp[:, None] == ids[None, :]
    per_step = lambda v: jnp.sum(jnp.where(is_exp, v[None, :], 0), axis=1).astype(I32)
    block_row = block_exp * (EXPERT_CAP // MOE_M) + (step - per_step(blk_start))
    has_rows = nblk > 0
    slot = (jnp.cumsum(has_rows.astype(I32)) - 1) % 2
    later = jnp.where(has_rows[None, :] & (ids[None, :] > ids[:, None]), ids[None, :], N_EXPERTS)
    nxt = jnp.min(later, axis=1)
    nxt = jnp.where(nxt == N_EXPERTS, -1, nxt)
    return (block_exp.astype(I32), block_row.astype(I32), per_step(slot), per_step(nxt),
            n_used.astype(I32).reshape(1))


def _gather_combine(tile, idx0_ref, idxn_ref, ys_hbm, x1_ref, g2_ref, gate_ref, o_ref, bufs, sem,
                    then=None):
    i = pl.program_id(0)
    n = pl.num_programs(0)
    rows = tile * TOP_K

    def wait_gather(s):
        pltpu.make_async_copy(ys_hbm.at[pl.ds(0, rows * ROW_TILE), :], bufs[s], sem.at[s]).wait()

    @pl.when(i == 0)
    def _():
        def body(r, c):
            for u in range(DMA_UNROLL):
                j = r * DMA_UNROLL + u
                _tile_copy(ys_hbm, idx0_ref[0, 0, j], bufs[0], j, sem.at[0]).start(priority=u % 2)
            return c

        lax.fori_loop(0, rows // DMA_UNROLL, body, 0)

    def step(cur, nxt):
        wait_gather(cur)
        for j in range(rows):
            _tile_copy(ys_hbm, idxn_ref[0, 0, j], bufs[nxt], j, sem.at[nxt]).start(priority=j % 2)
        gates = gate_ref[...]
        g2 = g2_ref[0]
        for c in range(ROW_TILE):
            cs = slice(c * LANES, (c + 1) * LANES)
            acc = jnp.zeros((tile, LANES), F32)
            for kk in range(TOP_K):
                acc = acc + gates[:, kk:kk + 1] * _load_row_tile_chunk(
                    bufs[cur], kk * tile * ROW_TILE, tile, c)
            o_ref[:, cs] = x1_ref[:, cs] + g2[:, cs] * acc
        if then is not None:
            then()

        @pl.when(i == n - 1)
        def _():
            wait_gather(nxt)

    @pl.when(i % 2 == 0)
    def _():
        step(0, 1)

    @pl.when(i % 2 == 1)
    def _():
        step(1, 0)


def _combine_kernel(idx0_ref, idxn_ref, ys_hbm, x1_ref, g2_ref, gate_ref, o_ref, buf0, buf1, sem):
    _gather_combine(TMC, idx0_ref, idxn_ref, ys_hbm, x1_ref, g2_ref, gate_ref, o_ref, (buf0, buf1), sem)


def _combine_inproj_kernel(idx0_ref, idxn_ref, ys_hbm, x1_ref, g2_ref, gate_ref,
                           sh_ref, sc_ref, g_ref, w_ref, seg_ref, qg_ref, kg_ref,
                           xn_ref, ug_ref, q_ref, k_ref, v_ref, buf0, buf1, sem):
    def inproj():
        _inproj_body(xn_ref[...], sh_ref, sc_ref, g_ref, w_ref, seg_ref, qg_ref, kg_ref,
                     ug_ref, q_ref, k_ref, v_ref)

    _gather_combine(TM, idx0_ref, idxn_ref, ys_hbm, x1_ref, g2_ref, gate_ref, xn_ref, (buf0, buf1), sem,
                    then=inproj)


def _choice_major(dest, tile):
    return dest.reshape(TOP_K, TOKENS // tile, tile).transpose(1, 0, 2).reshape(
        TOKENS // tile, 1, TOP_K * tile)


def _combine_inproj(dest, ys, x1, mod3_prev, gates, mod3, norm_g, w_in_bf, seg, qg, kg):
    n_tiles = TOKENS // TM
    rows = TM * TOP_K
    row = lambda i: (i, 0)
    const = lambda i: (0, 0)
    modspec = lambda j: pl.BlockSpec((1, 1, D_MODEL), lambda i: (i // TILES_PER_SEQ, 0, j))
    half_sd = jax.ShapeDtypeStruct((TOKENS, D_CONV), BF16)
    return pl.pallas_call(
        _combine_inproj_kernel,
        grid=(n_tiles,),
        in_specs=[
            pl.BlockSpec((1, 1, rows), lambda i: (0, 0, 0), memory_space=pltpu.SMEM),
            pl.BlockSpec((1, 1, rows), lambda i: (jnp.minimum(i + 1, n_tiles - 1), 0, 0),
                         memory_space=pltpu.SMEM),
            pl.BlockSpec(memory_space=pl.ANY),
            pl.BlockSpec((TM, D_MODEL), row),
            modspec(5),
            pl.BlockSpec((TM, TOP_K), row),
            modspec(0),
            modspec(1),
            pl.BlockSpec((1, D_MODEL), const),
            pl.BlockSpec((D_MODEL, D_IN), const),
            pl.BlockSpec((D_ATTN, D_ATTN), const),
            pl.BlockSpec((1, D_ATTN), const),
            pl.BlockSpec((1, D_ATTN), const),
        ],
        out_specs=[pl.BlockSpec((TM, D_MODEL), row)] + [pl.BlockSpec((TM, D_CONV), row)] * 4,
        out_shape=[jax.ShapeDtypeStruct((TOKENS, D_MODEL), F32)] + [half_sd] * 4,
        scratch_shapes=[
            pltpu.VMEM((rows * ROW_TILE, LANES), F32),
            pltpu.VMEM((rows * ROW_TILE, LANES), F32),
            pltpu.SemaphoreType.DMA((2,)),
        ],
        compiler_params=pltpu.CompilerParams(
            dimension_semantics=("arbitrary",), vmem_limit_bytes=VMEM_LIMIT),
        name="combine_inproj",
    )(_choice_major(dest, TM), _choice_major(dest, TM), ys, x1, mod3_prev, gates,
      mod3, mod3, norm_g, w_in_bf, seg, qg, kg)


def _combine(dest3, ys, x1, mod3, gates):
    n_tiles = TOKENS // TMC
    tiles_per_seq = SEQ // TMC
    rows = TMC * TOP_K
    return pl.pallas_call(
        _combine_kernel,
        grid=(n_tiles,),
        in_specs=[
            pl.BlockSpec((1, 1, rows), lambda i: (0, 0, 0), memory_space=pltpu.SMEM),
            pl.BlockSpec((1, 1, rows), lambda i: (jnp.minimum(i + 1, n_tiles - 1), 0, 0),
                         memory_space=pltpu.SMEM),
            pl.BlockSpec(memory_space=pl.ANY),
            pl.BlockSpec((TMC, D_MODEL), lambda i: (i, 0)),
            pl.BlockSpec((1, 1, D_MODEL), lambda i: (i // tiles_per_seq, 0, 5)),
            pl.BlockSpec((TMC, TOP_K), lambda i: (i, 0)),
        ],
        out_specs=pl.BlockSpec((TMC, D_MODEL), lambda i: (i, 0)),
        out_shape=jax.ShapeDtypeStruct((TOKENS, D_MODEL), F32),
        scratch_shapes=[
            pltpu.VMEM((rows * ROW_TILE, LANES), F32),
            pltpu.VMEM((rows * ROW_TILE, LANES), F32),
            pltpu.SemaphoreType.DMA((2,)),
        ],
        compiler_params=pltpu.CompilerParams(
            dimension_semantics=("arbitrary",), vmem_limit_bytes=VMEM_LIMIT),
        name="moe_combine",
    )(dest3, dest3, ys, x1, mod3, gates)


_I = np.arange(Q_PAIR)[:, None]
_K = np.arange(KEY_WIN)[None, :]
_BAND_OK = (_K // CHUNK >= _I // CHUNK) & (_K // CHUNK <= _I // CHUNK + LEFT_CHUNKS)
_SEG = np.kron(np.eye(N_HEADS), np.full((HEAD_DIM, HEAD_DIM), 1.0 / HEAD_DIM)).astype(np.float32)
_TRI = (np.arange(TM)[:, None] < np.arange(TM)[None, :]).astype(np.float32)


def _bias_tables(rel_bias):
    n_far = KEY_WIN - 1 + Q_PAIR - 2 * REL_CLIP
    ext = jnp.concatenate(
        [rel_bias[..., 1:], jnp.repeat(rel_bias[..., 2 * REL_CLIP:], n_far, axis=-1)], axis=-1)
    rev = ext[..., ::-1]
    period = KEY_WIN + Q_PAIR
    ring = jnp.concatenate(
        [rev[..., Q_PAIR - 1:], jnp.zeros(rev.shape[:-1] + (1,), rev.dtype), rev[..., :Q_PAIR - 1]],
        axis=-1)
    lead = rel_bias.shape[:-1]
    skew = jnp.tile(ring, Q_PAIR)[..., :Q_PAIR * (period - 1)].reshape(lead + (Q_PAIR, period - 1))
    table = jnp.where(_BAND_OK, skew[..., :KEY_WIN], NEG_INF).astype(F32)
    return table.reshape(DEPTH, N_HEADS // 2, 2, Q_PAIR, KEY_WIN).transpose(0, 1, 4, 2, 3).reshape(
        DEPTH, N_HEADS // 2, KEY_WIN, 2 * Q_PAIR)


def kernel(x, c, ada_w, ada_b, norm1_g, w_in, conv_w, conv_b, conv_ln_g, conv_ln_b, q_norm_g,
           k_norm_g, rel_bias, w_out, norm2_g, router_w, router_b, exp_w1, exp_b1, exp_w2, exp_b2):
    xf = x.reshape(TOKENS, D_MODEL)
    mod = _ada_mod(c, ada_w, ada_b)
    seg = jnp.asarray(_SEG, BF16)
    tri = jnp.asarray(_TRI, BF16)
    pad_e = ((0, 0), (0, LANES - N_EXPERTS))
    bias_tables = _bias_tables(rel_bias)
    moe = None
    for l in range(DEPTH):
        mod3 = mod[l].reshape(BATCH, 1, 6 * D_MODEL)
        qg = (jnp.tile(q_norm_g[l], N_HEADS) * (HEAD_DIM ** -0.5)).reshape(1, D_ATTN)
        kg = jnp.tile(k_norm_g[l], N_HEADS).reshape(1, D_ATTN)
        inproj_params = (mod3, norm1_g[l].reshape(1, D_MODEL), w_in[l].astype(BF16), seg, qg, kg)
        if moe is None:
            ug, q, k, v = _inproj(xf, *inproj_params)
        else:
            xf, ug, q, k, v = _combine_inproj(*moe, *inproj_params)
        y_conv = _conv(ug, conv_w[l], conv_b[l].reshape(1, D_CONV),
                       conv_ln_g[l].reshape(1, D_CONV), conv_ln_b[l].reshape(1, D_CONV))
        y_attn = _attention(q, k, v, bias_tables[l])
        w_out_bf = w_out[l].astype(BF16)
        rw_hi = router_w[l].astype(BF16)
        rw_lo = (router_w[l] - rw_hi.astype(F32)).astype(BF16)
        x1, dest, gates, cnt, xs = _outproj(
            y_conv, y_attn, xf, mod3, w_out_bf[:D_CONV], w_out_bf[D_CONV:],
            norm2_g[l].reshape(1, D_MODEL), jnp.pad(rw_hi, pad_e), jnp.pad(rw_lo, pad_e),
            jnp.pad(router_b[l].reshape(1, N_EXPERTS), pad_e), tri)
        ys = _experts(l, _block_tables(cnt[:, 0].astype(I32)), xs, exp_w1, exp_b1, exp_w2, exp_b2)
        moe = (dest, ys, x1, mod3, gates.T)
    dest, ys, x1, mod3, gates_t = moe
    xf = _combine(_choice_major(dest, TMC), ys, x1, mod3, gates_t)
    return xf.reshape(BATCH, SEQ, D_MODEL)
```

```python
import functools

import jax
import jax.numpy as jnp
import numpy as np
from jax import lax
from jax.experimental import pallas as pl
from jax.experimental.pallas import tpu as pltpu

D_MODEL = 1024
BATCH = 8
SEQ = 2048
DEPTH = 4
TOKENS = BATCH * SEQ

CHUNK = 64
D_CONV = D_MODEL // 2
D_ATTN = D_MODEL - D_CONV
N_HEADS = 8
HEAD_DIM = D_ATTN // N_HEADS
CONV_KERNEL = 31
LEFT_CHUNKS = 8
REL_CLIP = 128
N_EXPERTS = 32
TOP_K = 4
D_EXPERT = D_MODEL
SWIGLU_ALPHA = 1.702
SWIGLU_LIMIT = 7.0
EPS = 1e-6
D_IN = 2 * D_CONV + 3 * D_ATTN

F32 = jnp.float32
BF16 = jnp.bfloat16
I32 = jnp.int32
NEG_INF = float("-inf")

LANES = 128
SUBLANES = 8
TM = 512
TILES_PER_SEQ = SEQ // TM
Q_PAIR = 2 * CHUNK
KEY_WIN = (LEFT_CHUNKS + 2) * CHUNK
PAIRS_PER_TILE = TM // Q_PAIR
ATTN_SKEW = 3
HALO = 32
CONV_ROWS = 32
CONV_CHUNKS_PER_ITER = 4
MOE_M = 512
ZERO_ROWS = MOE_M // 2
N_BLOCKS = TOKENS * TOP_K // MOE_M + N_EXPERTS
EXPERT_CAP = TOKENS
ROW_TILE = D_MODEL // LANES
TMC = 256
GATHER_AHEAD = 2
DMA_UNROLL = 8

VMEM_LIMIT = 56 * 1024 * 1024


def _sigmoid(x):
    return 1.0 / (1.0 + jnp.exp(-x))


def _ada_kernel(c_ref, w_ref, b_ref, o_ref):
    c = c_ref[...]
    ca = (c * _sigmoid(c)).astype(BF16)
    o_ref[0] = jnp.dot(ca, w_ref[0].astype(BF16), preferred_element_type=F32) + b_ref[0]


def _ada_mod(c, ada_w, ada_b):
    n_tile = 1536
    return pl.pallas_call(
        _ada_kernel,
        grid=(DEPTH, 6 * D_MODEL // n_tile),
        in_specs=[
            pl.BlockSpec((BATCH, D_MODEL), lambda l, n: (0, 0)),
            pl.BlockSpec((1, D_MODEL, n_tile), lambda l, n: (l, 0, n)),
            pl.BlockSpec((1, 1, n_tile), lambda l, n: (l, 0, n)),
        ],
        out_specs=pl.BlockSpec((1, BATCH, n_tile), lambda l, n: (l, 0, n)),
        out_shape=jax.ShapeDtypeStruct((DEPTH, BATCH, 6 * D_MODEL), F32),
        compiler_params=pltpu.CompilerParams(
            dimension_semantics=("arbitrary", "arbitrary"), vmem_limit_bytes=VMEM_LIMIT),
        name="ada_mod",
    )(c, ada_w, ada_b.reshape(DEPTH, 1, 6 * D_MODEL))


def _modulated_norm(x, g, sc, sh):
    ms = jnp.mean(x * x, axis=-1, keepdims=True)
    return x * lax.rsqrt(ms + EPS) * g * (1.0 + sc) + sh


def _inproj_kernel(x_ref, *refs):
    _inproj_body(x_ref[...], *refs)


def _inproj_body(x, sh_ref, sc_ref, g_ref, w_ref, seg_ref, qg_ref, kg_ref, ug_ref, q_ref, k_ref, v_ref):
    h = _modulated_norm(x, g_ref[...], sc_ref[0], sh_ref[0])
    u = jnp.dot(h.astype(BF16), w_ref[...], preferred_element_type=F32)
    a = u[:, :D_CONV]
    gt = u[:, D_CONV:2 * D_CONV]
    ug_ref[...] = (a * _sigmoid(gt)).astype(BF16)
    o = 2 * D_CONV
    q = u[:, o:o + D_ATTN]
    k = u[:, o + D_ATTN:o + 2 * D_ATTN]
    v = u[:, o + 2 * D_ATTN:]
    qms = jnp.dot((q * q).astype(BF16), seg_ref[...], preferred_element_type=F32)
    kms = jnp.dot((k * k).astype(BF16), seg_ref[...], preferred_element_type=F32)
    q_ref[...] = (q * lax.rsqrt(qms + EPS) * qg_ref[...]).astype(BF16)
    k_ref[...] = (k * lax.rsqrt(kms + EPS) * kg_ref[...]).astype(BF16)
    v_ref[...] = v.astype(BF16)


def _inproj(x, mod3, norm_g, w_in_bf, seg, qg, kg):
    row = lambda i: (i, 0)
    const = lambda i: (0, 0)
    out_sd = jax.ShapeDtypeStruct((TOKENS, D_CONV), BF16)
    return pl.pallas_call(
        _inproj_kernel,
        grid=(TOKENS // TM,),
        in_specs=[
            pl.BlockSpec((TM, D_MODEL), row),
            pl.BlockSpec((1, 1, D_MODEL), lambda i: (i // TILES_PER_SEQ, 0, 0)),
            pl.BlockSpec((1, 1, D_MODEL), lambda i: (i // TILES_PER_SEQ, 0, 1)),
            pl.BlockSpec((1, D_MODEL), const),
            pl.BlockSpec((D_MODEL, D_IN), const),
            pl.BlockSpec((D_ATTN, D_ATTN), const),
            pl.BlockSpec((1, D_ATTN), const),
            pl.BlockSpec((1, D_ATTN), const),
        ],
        out_specs=[pl.BlockSpec((TM, D_CONV), row)] * 4,
        out_shape=[out_sd] * 4,
        compiler_params=pltpu.CompilerParams(
            dimension_semantics=("arbitrary",), vmem_limit_bytes=VMEM_LIMIT),
        name="inproj",
    )(x, mod3, mod3, norm_g, w_in_bf, seg, qg, kg)


def _conv_kernel(cur_ref, halo_ref, w_ref, b_ref, lg_ref, lb_ref, o_ref, xe_ref):
    i = pl.program_id(0)
    seq_start = (i % TILES_PER_SEQ) == 0
    halo = halo_ref[...].astype(F32)
    xe_ref[0, 0:HALO, :] = jnp.where(seq_start, 0.0, halo)
    xe_ref[0, HALO:, :] = cur_ref[...].astype(F32)
    ext = HALO + TM
    for s in range(1, SUBLANES):
        xe_ref[s, 0:ext - SUBLANES, :] = xe_ref[0, s:ext - SUBLANES + s, :]
    n_lane = D_CONV // LANES
    first_tap = HALO - (CONV_KERNEL - 1)

    lane_slices = [slice(c * LANES, (c + 1) * LANES) for c in range(n_lane)]

    def taps(r0):
        accs = [jnp.zeros((CONV_ROWS, LANES), F32) + b_ref[:, cs] for cs in lane_slices]
        for j in range(CONV_KERNEL):
            shift = (first_tap + j) % SUBLANES
            aligned = first_tap + j - shift
            for c, cs in enumerate(lane_slices):
                accs[c] = accs[c] + w_ref[j:j + 1, cs] * xe_ref[shift, pl.ds(r0 + aligned, CONV_ROWS), cs]
        return accs

    def norm_act(r0, accs):
        tot = accs[0]
        for c in range(1, n_lane):
            tot = tot + accs[c]
        mu = tot.sum(axis=-1, keepdims=True) * (1.0 / D_CONV)
        cen = [a - mu for a in accs]
        sq = cen[0] * cen[0]
        for c in range(1, n_lane):
            sq = sq + cen[c] * cen[c]
        inv = lax.rsqrt(sq.sum(axis=-1, keepdims=True) * (1.0 / D_CONV) + EPS)
        for c, cs in enumerate(lane_slices):
            y = cen[c] * inv * lg_ref[:, cs] + lb_ref[:, cs]
            o_ref[pl.ds(r0, CONV_ROWS), cs] = (y * _sigmoid(y)).astype(BF16)

    def body(rc, carry):
        pending = None
        for sub in range(CONV_CHUNKS_PER_ITER):
            r0 = pl.multiple_of((rc * CONV_CHUNKS_PER_ITER + sub) * CONV_ROWS, CONV_ROWS)
            accs = taps(r0)
            if pending is not None:
                norm_act(*pending)
            pending = (r0, accs)
        norm_act(*pending)
        return carry

    lax.fori_loop(0, TM // (CONV_ROWS * CONV_CHUNKS_PER_ITER), body, 0)


def _conv(ug, conv_w, conv_b, ln_g, ln_b):
    const = lambda i: (0, 0)
    halo_per_tile = TM // HALO
    return pl.pallas_call(
        _conv_kernel,
        grid=(TOKENS // TM,),
        in_specs=[
            pl.BlockSpec((TM, D_CONV), lambda i: (i, 0)),
            pl.BlockSpec((HALO, D_CONV), lambda i: (jnp.maximum(i * halo_per_tile - 1, 0), 0)),
            pl.BlockSpec((CONV_KERNEL, D_CONV), const),
            pl.BlockSpec((1, D_CONV), const),
            pl.BlockSpec((1, D_CONV), const),
            pl.BlockSpec((1, D_CONV), const),
        ],
        out_specs=pl.BlockSpec((TM, D_CONV), lambda i: (i, 0)),
        out_shape=jax.ShapeDtypeStruct((TOKENS, D_CONV), BF16),
        scratch_shapes=[pltpu.VMEM((SUBLANES, HALO + TM, D_CONV), F32)],
        compiler_params=pltpu.CompilerParams(
            dimension_semantics=("arbitrary",), vmem_limit_bytes=VMEM_LIMIT),
        name="conv_mixer",
    )(ug, ug, conv_w, conv_b, ln_g, ln_b)


def _attn_kernel(q_ref, kp_ref, kc_ref, vp_ref, vc_ref, bias_ref, o_ref, kw_ref, vt_ref):
    i = pl.program_id(0)
    seq_start = (i % TILES_PER_SEQ) == 0
    kw_ref[0:TM, :] = kp_ref[...]
    kw_ref[TM:, :] = kc_ref[...]
    vt_ref[:, 0:TM] = vp_ref[...].T
    vt_ref[:, TM:] = vc_ref[...].T
    pair = 2 * HEAD_DIM
    low = lax.broadcasted_iota(jnp.int32, (Q_PAIR, pair), 1) < HEAD_DIM

    def probs(cp, hp, k_lo):
        r0 = cp * Q_PAIR
        ps = slice(hp * pair, (hp + 1) * pair)
        q2 = q_ref[r0:r0 + Q_PAIR, ps]
        zero = jnp.zeros_like(q2)
        qm = jnp.concatenate([jnp.where(low, q2, zero), jnp.where(low, zero, q2)], axis=0)
        k2 = kw_ref[r0 + k_lo:r0 + KEY_WIN, ps]
        st = lax.dot_general(k2, qm, (((1,), (1,)), ((), ())), preferred_element_type=F32)
        st = st + bias_ref[hp, k_lo:, :]
        m = jnp.max(st, axis=0, keepdims=True)
        e = jnp.exp(st - m)
        return e.astype(BF16), 1.0 / jnp.sum(e, axis=0, keepdims=True)

    def weighted_values(cp, hp, k_lo, e, inv_l):
        r0 = cp * Q_PAIR
        ps = slice(hp * pair, (hp + 1) * pair)
        ot = jnp.dot(vt_ref[ps, r0 + k_lo:r0 + KEY_WIN], e,
                     preferred_element_type=F32) * inv_l
        o2t = jnp.concatenate([ot[:HEAD_DIM, :Q_PAIR], ot[HEAD_DIM:, Q_PAIR:]], axis=0)
        o_ref[r0:r0 + Q_PAIR, ps] = o2t.T.astype(BF16)

    def all_blocks(k_lo_of):
        pending = []
        for cp in range(PAIRS_PER_TILE):
            for hp in range(N_HEADS // 2):
                block = (cp, hp, k_lo_of(cp))
                pending.append(block + probs(*block))
                if len(pending) > ATTN_SKEW:
                    weighted_values(*pending.pop(0))
        for item in pending:
            weighted_values(*item)

    @pl.when(seq_start)
    def _():
        all_blocks(lambda cp: TM - cp * Q_PAIR)

    @pl.when(jnp.logical_not(seq_start))
    def _():
        all_blocks(lambda cp: 0)


def _attention(q, k, v, bias2):
    cur = lambda i: (i, 0)
    prev = lambda i: (jnp.maximum(i - 1, 0), 0)
    blk = (TM, D_ATTN)
    return pl.pallas_call(
        _attn_kernel,
        grid=(TOKENS // TM,),
        in_specs=[
            pl.BlockSpec(blk, cur),
            pl.BlockSpec(blk, prev), pl.BlockSpec(blk, cur),
            pl.BlockSpec(blk, prev), pl.BlockSpec(blk, cur),
            pl.BlockSpec((N_HEADS // 2, KEY_WIN, 2 * Q_PAIR), lambda i: (0, 0, 0)),
        ],
        out_specs=pl.BlockSpec(blk, cur),
        out_shape=jax.ShapeDtypeStruct((TOKENS, D_ATTN), BF16),
        scratch_shapes=[pltpu.VMEM((2 * TM, D_ATTN), BF16), pltpu.VMEM((D_ATTN, 2 * TM), BF16)],
        compiler_params=pltpu.CompilerParams(
            dimension_semantics=("arbitrary",), vmem_limit_bytes=VMEM_LIMIT),
        name="chunk_attention",
    )(q, k, k, v, v, bias2)


def _store_row_tiles(ref, base, rows, value):
    for c in range(ROW_TILE):
        ref[pl.ds(base + c, rows, stride=ROW_TILE), :] = value[:, c * LANES:(c + 1) * LANES]


def _load_row_tile_chunk(ref, base, rows, c):
    return ref[pl.ds(base + c, rows, stride=ROW_TILE), :]


def _tile_copy(src_ref, src_row, dst_ref, dst_row, sem):
    return pltpu.make_async_copy(
        src_ref.at[pl.ds(src_row * ROW_TILE, ROW_TILE), :],
        dst_ref.at[pl.ds(dst_row * ROW_TILE, ROW_TILE), :], sem)


def _outproj_kernel(yc_ref, ya_ref, x_ref, g1_ref, wt_ref, wb_ref, sh_ref, sc_ref, g_ref,
                    rwh_ref, rwl_ref, rb_ref, tri_ref,
                    x1_ref, dest_ref, gate_ref, cnt_ref, xs_hbm,
                    hbuf, dvm, dsm, cvm, csm, zbuf, carry, sem_idx, sem_push, sem_zero):
    i = pl.program_id(0)
    n = pl.num_programs(0)
    slot = i % 2

    @pl.when(i == 0)
    def _():
        carry[...] = jnp.zeros_like(carry)

    mixed = (jnp.dot(yc_ref[...], wt_ref[...], preferred_element_type=F32)
             + jnp.dot(ya_ref[...], wb_ref[...], preferred_element_type=F32))
    x1 = x_ref[...] + g1_ref[0] * mixed
    x1_ref[...] = x1
    h2 = _modulated_norm(x1, g_ref[...], sc_ref[0], sh_ref[0])

    hi = h2.astype(BF16)
    lo = (h2 - hi.astype(F32)).astype(BF16)
    logits = (jnp.dot(hi, rwh_ref[...], preferred_element_type=F32)
              + jnp.dot(hi, rwl_ref[...], preferred_element_type=F32)
              + jnp.dot(lo, rwh_ref[...], preferred_element_type=F32)) + rb_ref[...]
    lt = logits.T[:N_EXPERTS, :]

    eid = lax.broadcasted_iota(I32, (N_EXPERTS, TM), 0)
    picks, onehots, exps = [], [], []
    top0 = None
    for kk in range(TOP_K):
        m = jnp.max(lt, axis=0, keepdims=True)
        am = jnp.min(jnp.where(lt == m, eid, N_EXPERTS), axis=0, keepdims=True)
        oh = eid == am
        lt = jnp.where(oh, NEG_INF, lt)
        if kk == 0:
            top0 = m
        picks.append(am)
        onehots.append(oh)
        exps.append(jnp.exp(m - top0))
    denom = exps[0] + exps[1] + exps[2] + exps[3]
    gate_ref[...] = jnp.concatenate(exps, axis=0) * (1.0 / denom)

    oh_all = (onehots[0].astype(F32) + onehots[1].astype(F32)
              + onehots[2].astype(F32) + onehots[3].astype(F32))
    before = jnp.dot(oh_all.astype(BF16), tri_ref[...], preferred_element_type=F32) + carry[:, 0:1]
    dests = []
    for kk in range(TOP_K):
        rank = jnp.sum(jnp.where(onehots[kk], before, 0.0), axis=0, keepdims=True)
        dests.append(picks[kk] * EXPERT_CAP + rank.astype(I32))
    dest = jnp.concatenate(dests, axis=0)
    dest_ref[...] = dest
    carry[...] = carry[...] + jnp.sum(oh_all, axis=1, keepdims=True)
    cnt_ref[...] = carry[...]

    dvm[...] = jnp.concatenate([dest, jnp.zeros((SUBLANES - TOP_K, TM), I32)], axis=0)
    idx_copy = pltpu.make_async_copy(dvm, dsm, sem_idx)
    idx_copy.start()

    hb = pl.multiple_of(slot * (TM * ROW_TILE), TM * ROW_TILE)
    _store_row_tiles(hbuf, hb, TM, h2)

    def wait_push(s):
        base = pl.multiple_of(s * (TM * ROW_TILE), TM * ROW_TILE)
        for _ in range(TOP_K):
            pltpu.make_async_copy(hbuf.at[pl.ds(base, TM * ROW_TILE), :],
                                  xs_hbm.at[pl.ds(0, TM * ROW_TILE), :], sem_push.at[s]).wait()

    @pl.when(i > 0)
    def _():
        wait_push(1 - slot)

    idx_copy.wait()
    per_iter = DMA_UNROLL // TOP_K

    def push_body(r, c):
        for u in range(DMA_UNROLL):
            t = r * per_iter + u // TOP_K
            d = dsm[u % TOP_K, t]
            _tile_copy(hbuf, slot * TM + t, xs_hbm, d, sem_push.at[slot]).start(priority=u % 2)
        return c

    lax.fori_loop(0, TM // per_iter, push_body, 0)

    @pl.when(i == n - 1)
    def _():
        wait_push(slot)
        zbuf[...] = jnp.zeros_like(zbuf)
        cvm[...] = carry[...].astype(I32)
        cnt_copy = pltpu.make_async_copy(cvm, csm, sem_idx)
        cnt_copy.start()
        cnt_copy.wait()

        def per_expert(e, c, wait):
            cnt = csm[e, 0]
            pad = (MOE_M - cnt % MOE_M) % MOE_M
            for b in range(ZERO_ROWS.bit_length()):
                size = (1 << b) * ROW_TILE
                start = (e * EXPERT_CAP + cnt + (pad & ((1 << b) - 1))) * ROW_TILE

                @pl.when(((pad >> b) & 1) == 1)
                def _():
                    cp = pltpu.make_async_copy(zbuf.at[pl.ds(0, size), :],
                                               xs_hbm.at[pl.ds(start, size), :], sem_zero)
                    if wait:
                        cp.wait()
                    else:
                        cp.start()
            return c

        lax.fori_loop(0, N_EXPERTS, lambda e, c: per_expert(e, c, False), 0)
        lax.fori_loop(0, N_EXPERTS, lambda e, c: per_expert(e, c, True), 0)


def _outproj(yc, ya, x, mod3, w_top, w_bot, norm_g, rw_hi, rw_lo, rb, tri):
    row = lambda i: (i, 0)
    col = lambda i: (0, i)
    const = lambda i: (0, 0)
    modspec = lambda j: pl.BlockSpec((1, 1, D_MODEL), lambda i: (i // TILES_PER_SEQ, 0, j))
    return pl.pallas_call(
        _outproj_kernel,
        grid=(TOKENS // TM,),
        in_specs=[
            pl.BlockSpec((TM, D_CONV), row),
            pl.BlockSpec((TM, D_ATTN), row),
            pl.BlockSpec((TM, D_MODEL), row),
            modspec(2),
            pl.BlockSpec((D_CONV, D_MODEL), const),
            pl.BlockSpec((D_ATTN, D_MODEL), const),
            modspec(3),
            modspec(4),
            pl.BlockSpec((1, D_MODEL), const),
            pl.BlockSpec((D_MODEL, LANES), const),
            pl.BlockSpec((D_MODEL, LANES), const),
            pl.BlockSpec((1, LANES), const),
            pl.BlockSpec((TM, TM), const),
        ],
        out_specs=[
            pl.BlockSpec((TM, D_MODEL), row),
            pl.BlockSpec((TOP_K, TM), col),
            pl.BlockSpec((TOP_K, TM), col),
            pl.BlockSpec((N_EXPERTS, LANES), const),
            pl.BlockSpec(memory_space=pl.ANY),
        ],
        out_shape=[
            jax.ShapeDtypeStruct((TOKENS, D_MODEL), F32),
            jax.ShapeDtypeStruct((TOP_K, TOKENS), I32),
            jax.ShapeDtypeStruct((TOP_K, TOKENS), F32),
            jax.ShapeDtypeStruct((N_EXPERTS, LANES), F32),
            jax.ShapeDtypeStruct((N_EXPERTS * EXPERT_CAP * ROW_TILE, LANES), F32),
        ],
        scratch_shapes=[
            pltpu.VMEM((2 * TM * ROW_TILE, LANES), F32),
            pltpu.VMEM((SUBLANES, TM), I32),
            pltpu.SMEM((SUBLANES, TM), I32),
            pltpu.VMEM((N_EXPERTS, LANES), I32),
            pltpu.SMEM((N_EXPERTS, LANES), I32),
            pltpu.VMEM((ZERO_ROWS * ROW_TILE, LANES), F32),
            pltpu.VMEM((N_EXPERTS, LANES), F32),
            pltpu.SemaphoreType.DMA(()),
            pltpu.SemaphoreType.DMA((2,)),
            pltpu.SemaphoreType.DMA(()),
        ],
        compiler_params=pltpu.CompilerParams(
            dimension_semantics=("arbitrary",), vmem_limit_bytes=VMEM_LIMIT),
        name="outproj_router",
    )(yc, ya, x, mod3, w_top, w_bot, mod3, mod3, norm_g, rw_hi, rw_lo, rb, tri)


def _expert_kernel(layer, be_ref, br_ref, par_ref, nx_ref, nu_ref,
                   x_ref, w1_hbm, b1_ref, w2_hbm, b2_ref, o_ref,
                   w1f, w2f, w1b, w2b, sem_w):
    i = pl.program_id(0)
    e = be_ref[i]
    slot = par_ref[i]

    def weight_copies(expert, s):
        return (pltpu.make_async_copy(w1_hbm.at[layer, expert], w1f.at[s], sem_w.at[s]),
                pltpu.make_async_copy(w2_hbm.at[layer, expert], w2f.at[s], sem_w.at[s]))

    @pl.when(i < nu_ref[0])
    def _():
        prev_e = be_ref[jnp.maximum(i - 1, 0)]

        @pl.when(jnp.logical_or(i == 0, e != prev_e))
        def _():
            @pl.when(i == 0)
            def _():
                for cp in weight_copies(e, slot):
                    cp.start()

            for cp in weight_copies(e, slot):
                cp.wait()
            w1b[...] = w1f[slot].astype(BF16)
            w2b[...] = w2f[slot].astype(BF16)
            nxt = nx_ref[i]

            @pl.when(nxt >= 0)
            def _():
                for cp in weight_copies(nxt, 1 - slot):
                    cp.start()

        xs = jnp.concatenate(
            [_load_row_tile_chunk(x_ref, 0, MOE_M, c) for c in range(ROW_TILE)], axis=1).astype(BF16)
        u = jnp.dot(xs, w1b[...], preferred_element_type=F32) + b1_ref[0, 0]
        glu = jnp.minimum(u[:, :D_EXPERT], SWIGLU_LIMIT)
        lin = jnp.clip(u[:, D_EXPERT:], -SWIGLU_LIMIT, SWIGLU_LIMIT)
        act = glu * _sigmoid(SWIGLU_ALPHA * glu) * (lin + 1.0)
        y = jnp.dot(act.astype(BF16), w2b[...], preferred_element_type=F32) + b2_ref[0, 0]
        _store_row_tiles(o_ref, 0, MOE_M, y)


def _experts(layer, tables, xs, w1, b1, w2, b2):
    bmap = lambda i, be, br, par, nx, nu: (layer, be[i], 0, 0)
    rmap = lambda i, be, br, par, nx, nu: (br[i], 0)
    grid_spec = pltpu.PrefetchScalarGridSpec(
        num_scalar_prefetch=5,
        grid=(N_BLOCKS,),
        in_specs=[
            pl.BlockSpec((MOE_M * ROW_TILE, LANES), rmap),
            pl.BlockSpec(memory_space=pl.ANY),
            pl.BlockSpec((1, 1, 1, 2 * D_EXPERT), bmap),
            pl.BlockSpec(memory_space=pl.ANY),
            pl.BlockSpec((1, 1, 1, D_MODEL), bmap),
        ],
        out_specs=pl.BlockSpec((MOE_M * ROW_TILE, LANES), rmap),
        scratch_shapes=[
            pltpu.VMEM((2, D_MODEL, 2 * D_EXPERT), F32),
            pltpu.VMEM((2, D_EXPERT, D_MODEL), F32),
            pltpu.VMEM((D_MODEL, 2 * D_EXPERT), BF16),
            pltpu.VMEM((D_EXPERT, D_MODEL), BF16),
            pltpu.SemaphoreType.DMA((2,)),
        ],
    )
    return pl.pallas_call(
        functools.partial(_expert_kernel, layer),
        grid_spec=grid_spec,
        out_shape=jax.ShapeDtypeStruct((N_EXPERTS * EXPERT_CAP * ROW_TILE, LANES), F32),
        compiler_params=pltpu.CompilerParams(
            dimension_semantics=("arbitrary",), vmem_limit_bytes=VMEM_LIMIT),
        name="experts",
    )(*tables, xs, w1, b1.reshape(DEPTH, N_EXPERTS, 1, 2 * D_EXPERT), w2,
      b2.reshape(DEPTH, N_EXPERTS, 1, D_MODEL))


def _block_tables(counts):
    ids = jnp.arange(N_EXPERTS, dtype=I32)
    nblk = (counts + MOE_M - 1) // MOE_M
    blk_end = jnp.cumsum(nblk)
    blk_start = blk_end - nblk
    n_used = blk_end[-1]
    step = jnp.minimum(jnp.arange(N_BLOCKS, dtype=I32), n_used - 1)
    block_exp = jnp.minimum(
        jnp.sum((step[:, None] >= blk_end[None, :]).astype(I32), axis=1), N_EXPERTS - 1)
    is_exp = block_exp[:, None] == ids[None, :]
    per_step = lambda v: jnp.sum(jnp.where(is_exp, v[None, :], 0), axis=1).astype(I32)
    block_row = block_exp * (EXPERT_CAP // MOE_M) + (step - per_step(blk_start))
    has_rows = nblk > 0
    slot = (jnp.cumsum(has_rows.astype(I32)) - 1) % 2
    later = jnp.where(has_rows[None, :] & (ids[None, :] > ids[:, None]), ids[None, :], N_EXPERTS)
    nxt = jnp.min(later, axis=1)
    nxt = jnp.where(nxt == N_EXPERTS, -1, nxt)
    return (block_exp.astype(I32), block_row.astype(I32), per_step(slot), per_step(nxt),
            n_used.astype(I32).reshape(1))


def _gather_combine(tile, idx0_ref, idx1_ref, idxn_ref, ys_hbm, x1_ref, g2_ref, gate_ref, o_ref, bufs, sem,
                    then=None):
    i = pl.program_id(0)
    n = pl.num_programs(0)
    rows = tile * TOP_K
    n_buf = len(bufs)

    def wait_gather(s):
        pltpu.make_async_copy(ys_hbm.at[pl.ds(0, rows * ROW_TILE), :], bufs[s], sem.at[s]).wait()

    @pl.when(i == 0)
    def _():
        for s, idx_ref in enumerate((idx0_ref, idx1_ref)):
            def body(r, c, s=s, idx_ref=idx_ref):
                for u in range(DMA_UNROLL):
                    j = r * DMA_UNROLL + u
                    _tile_copy(ys_hbm, idx_ref[0, 0, j], bufs[s], j, sem.at[s]).start(priority=u % 2)
                return c

            lax.fori_loop(0, rows // DMA_UNROLL, body, 0)

    def step(cur, nxt):
        wait_gather(cur)
        for j in range(rows):
            _tile_copy(ys_hbm, idxn_ref[0, 0, j], bufs[nxt], j, sem.at[nxt]).start(priority=j % 2)
        gates = gate_ref[...]
        g2 = g2_ref[0]
        for c in range(ROW_TILE):
            cs = slice(c * LANES, (c + 1) * LANES)
            acc = jnp.zeros((tile, LANES), F32)
            for kk in range(TOP_K):
                acc = acc + gates[:, kk:kk + 1] * _load_row_tile_chunk(
                    bufs[cur], kk * tile * ROW_TILE, tile, c)
            o_ref[:, cs] = x1_ref[:, cs] + g2[:, cs] * acc
        if then is not None:
            then()

        @pl.when(i == n - 1)
        def _():
            for s in range(n_buf):
                if s != cur:
                    wait_gather(s)

    for r in range(n_buf):
        @pl.when(i % n_buf == r)
        def _(r=r):
            step(r, (r + GATHER_AHEAD) % n_buf)


def _combine_kernel(idx0_ref, idx1_ref, idxn_ref, ys_hbm, x1_ref, g2_ref, gate_ref, o_ref,
                    buf0, buf1, buf2, sem):
    _gather_combine(TMC, idx0_ref, idx1_ref, idxn_ref, ys_hbm, x1_ref, g2_ref, gate_ref, o_ref,
                    (buf0, buf1, buf2), sem)


def _combine_inproj_kernel(idx0_ref, idx1_ref, idxn_ref, ys_hbm, x1_ref, g2_ref, gate_ref,
                           sh_ref, sc_ref, g_ref, w_ref, seg_ref, qg_ref, kg_ref,
                           xn_ref, ug_ref, q_ref, k_ref, v_ref, buf0, buf1, buf2, sem):
    def inproj():
        _inproj_body(xn_ref[...], sh_ref, sc_ref, g_ref, w_ref, seg_ref, qg_ref, kg_ref,
                     ug_ref, q_ref, k_ref, v_ref)

    _gather_combine(TM, idx0_ref, idx1_ref, idxn_ref, ys_hbm, x1_ref, g2_ref, gate_ref, xn_ref,
                    (buf0, buf1, buf2), sem, then=inproj)


def _choice_major(dest, tile):
    return dest.reshape(TOP_K, TOKENS // tile, tile).transpose(1, 0, 2).reshape(
        TOKENS // tile, 1, TOP_K * tile)


def _combine_inproj(dest, ys, x1, mod3_prev, gates, mod3, norm_g, w_in_bf, seg, qg, kg):
    n_tiles = TOKENS // TM
    rows = TM * TOP_K
    row = lambda i: (i, 0)
    const = lambda i: (0, 0)
    modspec = lambda j: pl.BlockSpec((1, 1, D_MODEL), lambda i: (i // TILES_PER_SEQ, 0, j))
    half_sd = jax.ShapeDtypeStruct((TOKENS, D_CONV), BF16)
    return pl.pallas_call(
        _combine_inproj_kernel,
        grid=(n_tiles,),
        in_specs=[
            pl.BlockSpec((1, 1, rows), lambda i: (0, 0, 0), memory_space=pltpu.SMEM),
            pl.BlockSpec((1, 1, rows), lambda i: (1, 0, 0), memory_space=pltpu.SMEM),
            pl.BlockSpec((1, 1, rows), lambda i: (jnp.minimum(i + GATHER_AHEAD, n_tiles - 1), 0, 0),
                         memory_space=pltpu.SMEM),
            pl.BlockSpec(memory_space=pl.ANY),
            pl.BlockSpec((TM, D_MODEL), row),
            modspec(5),
            pl.BlockSpec((TM, TOP_K), row),
            modspec(0),
            modspec(1),
            pl.BlockSpec((1, D_MODEL), const),
            pl.BlockSpec((D_MODEL, D_IN), const),
            pl.BlockSpec((D_ATTN, D_ATTN), const),
            pl.BlockSpec((1, D_ATTN), const),
            pl.BlockSpec((1, D_ATTN), const),
        ],
        out_specs=[pl.BlockSpec((TM, D_MODEL), row)] + [pl.BlockSpec((TM, D_CONV), row)] * 4,
        out_shape=[jax.ShapeDtypeStruct((TOKENS, D_MODEL), F32)] + [half_sd] * 4,
        scratch_shapes=[
            pltpu.VMEM((rows * ROW_TILE, LANES), F32),
            pltpu.VMEM((rows * ROW_TILE, LANES), F32),
            pltpu.VMEM((rows * ROW_TILE, LANES), F32),
            pltpu.SemaphoreType.DMA((GATHER_AHEAD + 1,)),
        ],
        compiler_params=pltpu.CompilerParams(
            dimension_semantics=("arbitrary",), vmem_limit_bytes=VMEM_LIMIT),
        name="combine_inproj",
    )(*([_choice_major(dest, TM)] * 3), ys, x1, mod3_prev, gates,
      mod3, mod3, norm_g, w_in_bf, seg, qg, kg)


def _combine(dest3, ys, x1, mod3, gates):
    n_tiles = TOKENS // TMC
    tiles_per_seq = SEQ // TMC
    rows = TMC * TOP_K
    return pl.pallas_call(
        _combine_kernel,
        grid=(n_tiles,),
        in_specs=[
            pl.BlockSpec((1, 1, rows), lambda i: (0, 0, 0), memory_space=pltpu.SMEM),
            pl.BlockSpec((1, 1, rows), lambda i: (1, 0, 0), memory_space=pltpu.SMEM),
            pl.BlockSpec((1, 1, rows), lambda i: (jnp.minimum(i + GATHER_AHEAD, n_tiles - 1), 0, 0),
                         memory_space=pltpu.SMEM),
            pl.BlockSpec(memory_space=pl.ANY),
            pl.BlockSpec((TMC, D_MODEL), lambda i: (i, 0)),
            pl.BlockSpec((1, 1, D_MODEL), lambda i: (i // tiles_per_seq, 0, 5)),
            pl.BlockSpec((TMC, TOP_K), lambda i: (i, 0)),
        ],
        out_specs=pl.BlockSpec((TMC, D_MODEL), lambda i: (i, 0)),
        out_shape=jax.ShapeDtypeStruct((TOKENS, D_MODEL), F32),
        scratch_shapes=[
            pltpu.VMEM((rows * ROW_TILE, LANES), F32),
            pltpu.VMEM((rows * ROW_TILE, LANES), F32),
            pltpu.VMEM((rows * ROW_TILE, LANES), F32),
            pltpu.SemaphoreType.DMA((GATHER_AHEAD + 1,)),
        ],
        compiler_params=pltpu.CompilerParams(
            dimension_semantics=("arbitrary",), vmem_limit_bytes=VMEM_LIMIT),
        name="moe_combine",
    )(dest3, dest3, dest3, ys, x1, mod3, gates)


_I = np.arange(Q_PAIR)[:, None]
_K = np.arange(KEY_WIN)[None, :]
_BAND_OK = (_K // CHUNK >= _I // CHUNK) & (_K // CHUNK <= _I // CHUNK + LEFT_CHUNKS)
_SEG = np.kron(np.eye(N_HEADS), np.full((HEAD_DIM, HEAD_DIM), 1.0 / HEAD_DIM)).astype(np.float32)
_TRI = (np.arange(TM)[:, None] < np.arange(TM)[None, :]).astype(np.float32)


def _bias_tables(rel_bias):
    n_far = KEY_WIN - 1 + Q_PAIR - 2 * REL_CLIP
    ext = jnp.concatenate(
        [rel_bias[..., 1:], jnp.repeat(rel_bias[..., 2 * REL_CLIP:], n_far, axis=-1)], axis=-1)
    rev = ext[..., ::-1]
    period = KEY_WIN + Q_PAIR
    ring = jnp.concatenate(
        [rev[..., Q_PAIR - 1:], jnp.zeros(rev.shape[:-1] + (1,), rev.dtype), rev[..., :Q_PAIR - 1]],
        axis=-1)
    lead = rel_bias.shape[:-1]
    skew = jnp.tile(ring, Q_PAIR)[..., :Q_PAIR * (period - 1)].reshape(lead + (Q_PAIR, period - 1))
    table = jnp.where(_BAND_OK, skew[..., :KEY_WIN], NEG_INF).astype(F32)
    return table.reshape(DEPTH, N_HEADS // 2, 2, Q_PAIR, KEY_WIN).transpose(0, 1, 4, 2, 3).reshape(
        DEPTH, N_HEADS // 2, KEY_WIN, 2 * Q_PAIR)


def kernel(x, c, ada_w, ada_b, norm1_g, w_in, conv_w, conv_b, conv_ln_g, conv_ln_b, q_norm_g,
           k_norm_g, rel_bias, w_out, norm2_g, router_w, router_b, exp_w1, exp_b1, exp_w2, exp_b2):
    xf = x.reshape(TOKENS, D_MODEL)
    mod = _ada_mod(c, ada_w, ada_b)
    seg = jnp.asarray(_SEG, BF16)
    tri = jnp.asarray(_TRI, BF16)
    pad_e = ((0, 0), (0, LANES - N_EXPERTS))
    bias_tables = _bias_tables(rel_bias)
    moe = None
    for l in range(DEPTH):
        mod3 = mod[l].reshape(BATCH, 1, 6 * D_MODEL)
        qg = (jnp.tile(q_norm_g[l], N_HEADS) * (HEAD_DIM ** -0.5)).reshape(1, D_ATTN)
        kg = jnp.tile(k_norm_g[l], N_HEADS).reshape(1, D_ATTN)
        inproj_params = (mod3, norm1_g[l].reshape(1, D_MODEL), w_in[l].astype(BF16), seg, qg, kg)
        if moe is None:
            ug, q, k, v = _inproj(xf, *inproj_params)
        else:
            xf, ug, q, k, v = _combine_inproj(*moe, *inproj_params)
        y_conv = _conv(ug, conv_w[l], conv_b[l].reshape(1, D_CONV),
                       conv_ln_g[l].reshape(1, D_CONV), conv_ln_b[l].reshape(1, D_CONV))
        y_attn = _attention(q, k, v, bias_tables[l])
        w_out_bf = w_out[l].astype(BF16)
        rw_hi = router_w[l].astype(BF16)
        rw_lo = (router_w[l] - rw_hi.astype(F32)).astype(BF16)
        x1, dest, gates, cnt, xs = _outproj(
            y_conv, y_attn, xf, mod3, w_out_bf[:D_CONV], w_out_bf[D_CONV:],
            norm2_g[l].reshape(1, D_MODEL), jnp.pad(rw_hi, pad_e), jnp.pad(rw_lo, pad_e),
            jnp.pad(router_b[l].reshape(1, N_EXPERTS), pad_e), tri)
        ys = _experts(l, _block_tables(cnt[:, 0].astype(I32)), xs, exp_w1, exp_b1, exp_w2, exp_b2)
        moe = (dest, ys, x1, mod3, gates.T)
    dest, ys, x1, mod3, gates_t = moe
    xf = _combine(_choice_major(dest, TMC), ys, x1, mod3, gates_t)
    return xf.reshape(BATCH, SEQ, D_MODEL)
```

```python
import functools

import jax
import jax.numpy as jnp
import numpy as np
from jax import lax
from jax.experimental import pallas as pl
from jax.experimental.pallas import tpu as pltpu

D_MODEL = 1024
BATCH = 8
SEQ = 2048
DEPTH = 4
TOKENS = BATCH * SEQ

CHUNK = 64
D_CONV = D_MODEL // 2
D_ATTN = D_MODEL - D_CONV
N_HEADS = 8
HEAD_DIM = D_ATTN // N_HEADS
CONV_KERNEL = 31
LEFT_CHUNKS = 8
REL_CLIP = 128
N_EXPERTS = 32
TOP_K = 4
D_EXPERT = D_MODEL
SWIGLU_ALPHA = 1.702
SWIGLU_LIMIT = 7.0
EPS = 1e-6
D_IN = 2 * D_CONV + 3 * D_ATTN

F32 = jnp.float32
BF16 = jnp.bfloat16
I32 = jnp.int32
NEG_INF = float("-inf")

LANES = 128
SUBLANES = 8
TM = 512
TILES_PER_SEQ = SEQ // TM
Q_PAIR = 2 * CHUNK
KEY_WIN = (LEFT_CHUNKS + 2) * CHUNK
PAIRS_PER_TILE = TM // Q_PAIR
ATTN_SKEW = 3
HALO = 32
CONV_ROWS = 32
CONV_CHUNKS_PER_ITER = 4
MOE_M = 512
ZERO_ROWS = MOE_M // 2
N_BLOCKS = TOKENS * TOP_K // MOE_M + N_EXPERTS
EXPERT_CAP = TOKENS
ROW_TILE = D_MODEL // LANES
TMC = 256
GATHER_AHEAD = 2
DMA_UNROLL = 8

VMEM_LIMIT = 56 * 1024 * 1024


def _sigmoid(x):
    return 1.0 / (1.0 + jnp.exp(-x))


def _ada_kernel(c_ref, w_ref, b_ref, o_ref):
    c = c_ref[...]
    ca = (c * _sigmoid(c)).astype(BF16)
    o_ref[0] = jnp.dot(ca, w_ref[0].astype(BF16), preferred_element_type=F32) + b_ref[0]


def _ada_mod(c, ada_w, ada_b):
    n_tile = 1536
    return pl.pallas_call(
        _ada_kernel,
        grid=(DEPTH, 6 * D_MODEL // n_tile),
        in_specs=[
            pl.BlockSpec((BATCH, D_MODEL), lambda l, n: (0, 0)),
            pl.BlockSpec((1, D_MODEL, n_tile), lambda l, n: (l, 0, n)),
            pl.BlockSpec((1, 1, n_tile), lambda l, n: (l, 0, n)),
        ],
        out_specs=pl.BlockSpec((1, BATCH, n_tile), lambda l, n: (l, 0, n)),
        out_shape=jax.ShapeDtypeStruct((DEPTH, BATCH, 6 * D_MODEL), F32),
        compiler_params=pltpu.CompilerParams(
            dimension_semantics=("arbitrary", "arbitrary"), vmem_limit_bytes=VMEM_LIMIT),
        name="ada_mod",
    )(c, ada_w, ada_b.reshape(DEPTH, 1, 6 * D_MODEL))


def _modulated_norm(x, g, sc, sh):
    ms = jnp.mean(x * x, axis=-1, keepdims=True)
    return x * lax.rsqrt(ms + EPS) * g * (1.0 + sc) + sh


def _inproj_kernel(x_ref, *refs):
    _inproj_body(x_ref[...], *refs)


def _inproj_body(x, sh_ref, sc_ref, g_ref, w_ref, seg_ref, qg_ref, kg_ref, ug_ref, q_ref, k_ref, v_ref):
    h = _modulated_norm(x, g_ref[...], sc_ref[0], sh_ref[0])
    u = jnp.dot(h.astype(BF16), w_ref[...], preferred_element_type=F32)
    a = u[:, :D_CONV]
    gt = u[:, D_CONV:2 * D_CONV]
    ug_ref[...] = (a * _sigmoid(gt)).astype(BF16)
    o = 2 * D_CONV
    q = u[:, o:o + D_ATTN]
    k = u[:, o + D_ATTN:o + 2 * D_ATTN]
    v = u[:, o + 2 * D_ATTN:]
    qms = jnp.dot((q * q).astype(BF16), seg_ref[...], preferred_element_type=F32)
    kms = jnp.dot((k * k).astype(BF16), seg_ref[...], preferred_element_type=F32)
    q_ref[...] = (q * lax.rsqrt(qms + EPS) * qg_ref[...]).astype(BF16)
    k_ref[...] = (k * lax.rsqrt(kms + EPS) * kg_ref[...]).astype(BF16)
    v_ref[...] = v.astype(BF16)


def _inproj(x, mod3, norm_g, w_in_bf, seg, qg, kg):
    row = lambda i: (i, 0)
    const = lambda i: (0, 0)
    out_sd = jax.ShapeDtypeStruct((TOKENS, D_CONV), BF16)
    return pl.pallas_call(
        _inproj_kernel,
        grid=(TOKENS // TM,),
        in_specs=[
            pl.BlockSpec((TM, D_MODEL), row),
            pl.BlockSpec((1, 1, D_MODEL), lambda i: (i // TILES_PER_SEQ, 0, 0)),
            pl.BlockSpec((1, 1, D_MODEL), lambda i: (i // TILES_PER_SEQ, 0, 1)),
            pl.BlockSpec((1, D_MODEL), const),
            pl.BlockSpec((D_MODEL, D_IN), const),
            pl.BlockSpec((D_ATTN, D_ATTN), const),
            pl.BlockSpec((1, D_ATTN), const),
            pl.BlockSpec((1, D_ATTN), const),
        ],
        out_specs=[pl.BlockSpec((TM, D_CONV), row)] * 4,
        out_shape=[out_sd] * 4,
        compiler_params=pltpu.CompilerParams(
            dimension_semantics=("arbitrary",), vmem_limit_bytes=VMEM_LIMIT),
        name="inproj",
    )(x, mod3, mod3, norm_g, w_in_bf, seg, qg, kg)


def _conv_kernel(cur_ref, halo_ref, w_ref, b_ref, lg_ref, lb_ref, o_ref, xe_ref):
    i = pl.program_id(0)
    seq_start = (i % TILES_PER_SEQ) == 0
    halo = halo_ref[...].astype(F32)
    xe_ref[0, 0:HALO, :] = jnp.where(seq_start, 0.0, halo)
    xe_ref[0, HALO:, :] = cur_ref[...].astype(F32)
    ext = HALO + TM
    for s in range(1, SUBLANES):
        xe_ref[s, 0:ext - SUBLANES, :] = xe_ref[0, s:ext - SUBLANES + s, :]
    n_lane = D_CONV // LANES
    first_tap = HALO - (CONV_KERNEL - 1)

    lane_slices = [slice(c * LANES, (c + 1) * LANES) for c in range(n_lane)]

    def taps(r0):
        accs = [jnp.zeros((CONV_ROWS, LANES), F32) + b_ref[:, cs] for cs in lane_slices]
        for j in range(CONV_KERNEL):
            shift = (first_tap + j) % SUBLANES
            aligned = first_tap + j - shift
            for c, cs in enumerate(lane_slices):
                accs[c] = accs[c] + w_ref[j:j + 1, cs] * xe_ref[shift, pl.ds(r0 + aligned, CONV_ROWS), cs]
        return accs

    def norm_act(r0, accs):
        tot = accs[0]
        for c in range(1, n_lane):
            tot = tot + accs[c]
        mu = tot.sum(axis=-1, keepdims=True) * (1.0 / D_CONV)
        cen = [a - mu for a in accs]
        sq = cen[0] * cen[0]
        for c in range(1, n_lane):
            sq = sq + cen[c] * cen[c]
        inv = lax.rsqrt(sq.sum(axis=-1, keepdims=True) * (1.0 / D_CONV) + EPS)
        for c, cs in enumerate(lane_slices):
            y = cen[c] * inv * lg_ref[:, cs] + lb_ref[:, cs]
            o_ref[pl.ds(r0, CONV_ROWS), cs] = (y * _sigmoid(y)).astype(BF16)

    def body(rc, carry):
        pending = None
        for sub in range(CONV_CHUNKS_PER_ITER):
            r0 = pl.multiple_of((rc * CONV_CHUNKS_PER_ITER + sub) * CONV_ROWS, CONV_ROWS)
            accs = taps(r0)
            if pending is not None:
                norm_act(*pending)
            pending = (r0, accs)
        norm_act(*pending)
        return carry

    lax.fori_loop(0, TM // (CONV_ROWS * CONV_CHUNKS_PER_ITER), body, 0)


def _conv(ug, conv_w, conv_b, ln_g, ln_b):
    const = lambda i: (0, 0)
    halo_per_tile = TM // HALO
    return pl.pallas_call(
        _conv_kernel,
        grid=(TOKENS // TM,),
        in_specs=[
            pl.BlockSpec((TM, D_CONV), lambda i: (i, 0)),
            pl.BlockSpec((HALO, D_CONV), lambda i: (jnp.maximum(i * halo_per_tile - 1, 0), 0)),
            pl.BlockSpec((CONV_KERNEL, D_CONV), const),
            pl.BlockSpec((1, D_CONV), const),
            pl.BlockSpec((1, D_CONV), const),
            pl.BlockSpec((1, D_CONV), const),
        ],
        out_specs=pl.BlockSpec((TM, D_CONV), lambda i: (i, 0)),
        out_shape=jax.ShapeDtypeStruct((TOKENS, D_CONV), BF16),
        scratch_shapes=[pltpu.VMEM((SUBLANES, HALO + TM, D_CONV), F32)],
        compiler_params=pltpu.CompilerParams(
            dimension_semantics=("arbitrary",), vmem_limit_bytes=VMEM_LIMIT),
        name="conv_mixer",
    )(ug, ug, conv_w, conv_b, ln_g, ln_b)


def _attn_kernel(q_ref, kp_ref, kc_ref, vp_ref, vc_ref, bias_ref, o_ref, kw_ref, vt_ref):
    i = pl.program_id(0)
    seq_start = (i % TILES_PER_SEQ) == 0
    kw_ref[0:TM, :] = kp_ref[...]
    kw_ref[TM:, :] = kc_ref[...]
    vt_ref[:, 0:TM] = vp_ref[...].T
    vt_ref[:, TM:] = vc_ref[...].T
    pair = 2 * HEAD_DIM
    low = lax.broadcasted_iota(jnp.int32, (Q_PAIR, pair), 1) < HEAD_DIM

    def probs(cp, hp, k_lo):
        r0 = cp * Q_PAIR
        ps = slice(hp * pair, (hp + 1) * pair)
        q2 = q_ref[r0:r0 + Q_PAIR, ps]
        zero = jnp.zeros_like(q2)
        qm = jnp.concatenate([jnp.where(low, q2, zero), jnp.where(low, zero, q2)], axis=0)
        k2 = kw_ref[r0 + k_lo:r0 + KEY_WIN, ps]
        st = lax.dot_general(k2, qm, (((1,), (1,)), ((), ())), preferred_element_type=F32)
        st = st + bias_ref[hp, k_lo:, :]
        m = jnp.max(st, axis=0, keepdims=True)
        e = jnp.exp(st - m)
        return e.astype(BF16), 1.0 / jnp.sum(e, axis=0, keepdims=True)

    def weighted_values(cp, hp, k_lo, e, inv_l):
        r0 = cp * Q_PAIR
        ps = slice(hp * pair, (hp + 1) * pair)
        ot = jnp.dot(vt_ref[ps, r0 + k_lo:r0 + KEY_WIN], e,
                     preferred_element_type=F32) * inv_l
        o2t = jnp.concatenate([ot[:HEAD_DIM, :Q_PAIR], ot[HEAD_DIM:, Q_PAIR:]], axis=0)
        o_ref[r0:r0 + Q_PAIR, ps] = o2t.T.astype(BF16)

    def all_blocks(k_lo_of):
        pending = []
        for cp in range(PAIRS_PER_TILE):
            for hp in range(N_HEADS // 2):
                block = (cp, hp, k_lo_of(cp))
                pending.append(block + probs(*block))
                if len(pending) > ATTN_SKEW:
                    weighted_values(*pending.pop(0))
        for item in pending:
            weighted_values(*item)

    @pl.when(seq_start)
    def _():
        all_blocks(lambda cp: TM - cp * Q_PAIR)

    @pl.when(jnp.logical_not(seq_start))
    def _():
        all_blocks(lambda cp: 0)


def _attention(q, k, v, bias2):
    cur = lambda i: (i, 0)
    prev = lambda i: (jnp.maximum(i - 1, 0), 0)
    blk = (TM, D_ATTN)
    return pl.pallas_call(
        _attn_kernel,
        grid=(TOKENS // TM,),
        in_specs=[
            pl.BlockSpec(blk, cur),
            pl.BlockSpec(blk, prev), pl.BlockSpec(blk, cur),
            pl.BlockSpec(blk, prev), pl.BlockSpec(blk, cur),
            pl.BlockSpec((N_HEADS // 2, KEY_WIN, 2 * Q_PAIR), lambda i: (0, 0, 0)),
        ],
        out_specs=pl.BlockSpec(blk, cur),
        out_shape=jax.ShapeDtypeStruct((TOKENS, D_ATTN), BF16),
        scratch_shapes=[pltpu.VMEM((2 * TM, D_ATTN), BF16), pltpu.VMEM((D_ATTN, 2 * TM), BF16)],
        compiler_params=pltpu.CompilerParams(
            dimension_semantics=("arbitrary",), vmem_limit_bytes=VMEM_LIMIT),
        name="chunk_attention",
    )(q, k, k, v, v, bias2)


def _store_row_tiles(ref, base, rows, value):
    for c in range(ROW_TILE):
        ref[pl.ds(base + c, rows, stride=ROW_TILE), :] = value[:, c * LANES:(c + 1) * LANES]


def _load_row_tile_chunk(ref, base, rows, c):
    return ref[pl.ds(base + c, rows, stride=ROW_TILE), :]


def _tile_copy(src_ref, src_row, dst_ref, dst_row, sem):
    return pltpu.make_async_copy(
        src_ref.at[pl.ds(src_row * ROW_TILE, ROW_TILE), :],
        dst_ref.at[pl.ds(dst_row * ROW_TILE, ROW_TILE), :], sem)


def _outproj_kernel(yc_ref, ya_ref, x_ref, g1_ref, wt_ref, wb_ref, sh_ref, sc_ref, g_ref,
                    rwh_ref, rwl_ref, rb_ref, tri_ref,
                    x1_ref, dest_ref, gate_ref, cnt_ref, xs_hbm,
                    hbuf0, hbuf1, dvm, dsm0, dsm1, cvm, csm, zbuf, carry, sem_idx, sem_push, sem_zero):
    i = pl.program_id(0)
    n = pl.num_programs(0)
    hbufs = (hbuf0, hbuf1)
    dsms = (dsm0, dsm1)

    @pl.when(i == 0)
    def _():
        carry[...] = jnp.zeros_like(carry)

    def wait_push(s):
        for _ in range(TOP_K):
            pltpu.make_async_copy(hbufs[s], xs_hbm.at[pl.ds(0, TM * ROW_TILE), :], sem_push.at[s]).wait()

    def idx_copy(s):
        return pltpu.make_async_copy(dvm, dsms[s], sem_idx)

    for s in range(2):
        @pl.when(jnp.logical_and(i >= 2, i % 2 == s))
        def _(s=s):
            wait_push(s)

    def step(p, push_prev):
        q = 1 - p
        if push_prev:
            idx_copy(q).wait()
            for t in range(TM):
                for kk in range(TOP_K):
                    _tile_copy(hbufs[q], t, xs_hbm, dsms[q][kk, t], sem_push.at[q]).start(priority=kk % 2)

        mixed = (jnp.dot(yc_ref[...], wt_ref[...], preferred_element_type=F32)
                 + jnp.dot(ya_ref[...], wb_ref[...], preferred_element_type=F32))
        x1 = x_ref[...] + g1_ref[0] * mixed
        x1_ref[...] = x1
        h2 = _modulated_norm(x1, g_ref[...], sc_ref[0], sh_ref[0])

        hi = h2.astype(BF16)
        lo = (h2 - hi.astype(F32)).astype(BF16)
        logits = (jnp.dot(hi, rwh_ref[...], preferred_element_type=F32)
                  + jnp.dot(hi, rwl_ref[...], preferred_element_type=F32)
                  + jnp.dot(lo, rwh_ref[...], preferred_element_type=F32)) + rb_ref[...]
        lt = logits.T[:N_EXPERTS, :]

        eid = lax.broadcasted_iota(I32, (N_EXPERTS, TM), 0)
        picks, onehots, exps = [], [], []
        top0 = None
        for kk in range(TOP_K):
            m = jnp.max(lt, axis=0, keepdims=True)
            am = jnp.min(jnp.where(lt == m, eid, N_EXPERTS), axis=0, keepdims=True)
            oh = eid == am
            lt = jnp.where(oh, NEG_INF, lt)
            if kk == 0:
                top0 = m
            picks.append(am)
            onehots.append(oh)
            exps.append(jnp.exp(m - top0))
        denom = exps[0] + exps[1] + exps[2] + exps[3]
        gate_ref[...] = jnp.concatenate(exps, axis=0) * (1.0 / denom)

        oh_all = (onehots[0].astype(F32) + onehots[1].astype(F32)
                  + onehots[2].astype(F32) + onehots[3].astype(F32))
        before = jnp.dot(oh_all.astype(BF16), tri_ref[...], preferred_element_type=F32) + carry[:, 0:1]
        dests = []
        for kk in range(TOP_K):
            rank = jnp.sum(jnp.where(onehots[kk], before, 0.0), axis=0, keepdims=True)
            dests.append(picks[kk] * EXPERT_CAP + rank.astype(I32))
        dest = jnp.concatenate(dests, axis=0)
        dest_ref[...] = dest
        carry[...] = carry[...] + jnp.sum(oh_all, axis=1, keepdims=True)
        cnt_ref[...] = carry[...]

        _store_row_tiles(hbufs[p], 0, TM, h2)
        dvm[...] = jnp.concatenate([dest, jnp.zeros((SUBLANES - TOP_K, TM), I32)], axis=0)
        idx_copy(p).start()

        @pl.when(i == n - 1)
        def _():
            idx_copy(p).wait()
            per_iter = DMA_UNROLL // TOP_K

            def push_body(r, c):
                for u in range(DMA_UNROLL):
                    t = r * per_iter + u // TOP_K
                    _tile_copy(hbufs[p], t, xs_hbm, dsms[p][u % TOP_K, t],
                               sem_push.at[p]).start(priority=u % 2)
                return c

            lax.fori_loop(0, TM // per_iter, push_body, 0)
            wait_push(q)
            wait_push(p)

    @pl.when(i == 0)
    def _():
        step(0, False)

    @pl.when(i % 2 == 1)
    def _():
        step(1, True)

    @pl.when(jnp.logical_and(i > 0, i % 2 == 0))
    def _():
        step(0, True)

    @pl.when(i == n - 1)
    def _():
        zbuf[...] = jnp.zeros_like(zbuf)
        cvm[...] = carry[...].astype(I32)
        cnt_copy = pltpu.make_async_copy(cvm, csm, sem_idx)
        cnt_copy.start()
        cnt_copy.wait()

        def per_expert(e, c, wait):
            cnt = csm[e, 0]
            pad = (MOE_M - cnt % MOE_M) % MOE_M
            for b in range(ZERO_ROWS.bit_length()):
                size = (1 << b) * ROW_TILE
                start = (e * EXPERT_CAP + cnt + (pad & ((1 << b) - 1))) * ROW_TILE

                @pl.when(((pad >> b) & 1) == 1)
                def _():
                    cp = pltpu.make_async_copy(zbuf.at[pl.ds(0, size), :],
                                               xs_hbm.at[pl.ds(start, size), :], sem_zero)
                    if wait:
                        cp.wait()
                    else:
                        cp.start()
            return c

        lax.fori_loop(0, N_EXPERTS, lambda e, c: per_expert(e, c, False), 0)
        lax.fori_loop(0, N_EXPERTS, lambda e, c: per_expert(e, c, True), 0)


def _outproj(yc, ya, x, mod3, w_top, w_bot, norm_g, rw_hi, rw_lo, rb, tri):
    row = lambda i: (i, 0)
    col = lambda i: (0, i)
    const = lambda i: (0, 0)
    modspec = lambda j: pl.BlockSpec((1, 1, D_MODEL), lambda i: (i // TILES_PER_SEQ, 0, j))
    return pl.pallas_call(
        _outproj_kernel,
        grid=(TOKENS // TM,),
        in_specs=[
            pl.BlockSpec((TM, D_CONV), row),
            pl.BlockSpec((TM, D_ATTN), row),
            pl.BlockSpec((TM, D_MODEL), row),
            modspec(2),
            pl.BlockSpec((D_CONV, D_MODEL), const),
            pl.BlockSpec((D_ATTN, D_MODEL), const),
            modspec(3),
            modspec(4),
            pl.BlockSpec((1, D_MODEL), const),
            pl.BlockSpec((D_MODEL, LANES), const),
            pl.BlockSpec((D_MODEL, LANES), const),
            pl.BlockSpec((1, LANES), const),
            pl.BlockSpec((TM, TM), const),
        ],
        out_specs=[
            pl.BlockSpec((TM, D_MODEL), row),
            pl.BlockSpec((TOP_K, TM), col),
            pl.BlockSpec((TOP_K, TM), col),
            pl.BlockSpec((N_EXPERTS, LANES), const),
            pl.BlockSpec(memory_space=pl.ANY),
        ],
        out_shape=[
            jax.ShapeDtypeStruct((TOKENS, D_MODEL), F32),
            jax.ShapeDtypeStruct((TOP_K, TOKENS), I32),
            jax.ShapeDtypeStruct((TOP_K, TOKENS), F32),
            jax.ShapeDtypeStruct((N_EXPERTS, LANES), F32),
            jax.ShapeDtypeStruct((N_EXPERTS * EXPERT_CAP * ROW_TILE, LANES), F32),
        ],
        scratch_shapes=[
            pltpu.VMEM((TM * ROW_TILE, LANES), F32),
            pltpu.VMEM((TM * ROW_TILE, LANES), F32),
            pltpu.VMEM((SUBLANES, TM), I32),
            pltpu.SMEM((SUBLANES, TM), I32),
            pltpu.SMEM((SUBLANES, TM), I32),
            pltpu.VMEM((N_EXPERTS, LANES), I32),
            pltpu.SMEM((N_EXPERTS, LANES), I32),
            pltpu.VMEM((ZERO_ROWS * ROW_TILE, LANES), F32),
            pltpu.VMEM((N_EXPERTS, LANES), F32),
            pltpu.SemaphoreType.DMA(()),
            pltpu.SemaphoreType.DMA((2,)),
            pltpu.SemaphoreType.DMA(()),
        ],
        compiler_params=pltpu.CompilerParams(
            dimension_semantics=("arbitrary",), vmem_limit_bytes=VMEM_LIMIT),
        name="outproj_router",
    )(yc, ya, x, mod3, w_top, w_bot, mod3, mod3, norm_g, rw_hi, rw_lo, rb, tri)


def _expert_kernel(layer, be_ref, br_ref, par_ref, nx_ref, nu_ref,
                   x_ref, w1_hbm, b1_ref, w2_hbm, b2_ref, o_ref,
                   w1f, w2f, w1b, w2b, sem_w):
    i = pl.program_id(0)
    e = be_ref[i]
    slot = par_ref[i]

    def weight_copies(expert, s):
        return (pltpu.make_async_copy(w1_hbm.at[layer, expert], w1f.at[s], sem_w.at[s]),
                pltpu.make_async_copy(w2_hbm.at[layer, expert], w2f.at[s], sem_w.at[s]))

    @pl.when(i < nu_ref[0])
    def _():
        prev_e = be_ref[jnp.maximum(i - 1, 0)]

        @pl.when(jnp.logical_or(i == 0, e != prev_e))
        def _():
            @pl.when(i == 0)
            def _():
                for cp in weight_copies(e, slot):
                    cp.start()

            for cp in weight_copies(e, slot):
                cp.wait()
            w1b[...] = w1f[slot].astype(BF16)
            w2b[...] = w2f[slot].astype(BF16)
            nxt = nx_ref[i]

            @pl.when(nxt >= 0)
            def _():
                for cp in weight_copies(nxt, 1 - slot):
                    cp.start()

        xs = jnp.concatenate(
            [_load_row_tile_chunk(x_ref, 0, MOE_M, c) for c in range(ROW_TILE)], axis=1).astype(BF16)
        u = jnp.dot(xs, w1b[...], preferred_element_type=F32) + b1_ref[0, 0]
        glu = jnp.minimum(u[:, :D_EXPERT], SWIGLU_LIMIT)
        lin = jnp.clip(u[:, D_EXPERT:], -SWIGLU_LIMIT, SWIGLU_LIMIT)
        act = glu * _sigmoid(SWIGLU_ALPHA * glu) * (lin + 1.0)
        y = jnp.dot(act.astype(BF16), w2b[...], preferred_element_type=F32) + b2_ref[0, 0]
        _store_row_tiles(o_ref, 0, MOE_M, y)


def _experts(layer, tables, xs, w1, b1, w2, b2):
    bmap = lambda i, be, br, par, nx, nu: (layer, be[i], 0, 0)
    rmap = lambda i, be, br, par, nx, nu: (br[i], 0)
    grid_spec = pltpu.PrefetchScalarGridSpec(
        num_scalar_prefetch=5,
        grid=(N_BLOCKS,),
        in_specs=[
            pl.BlockSpec((MOE_M * ROW_TILE, LANES), rmap),
            pl.BlockSpec(memory_space=pl.ANY),
            pl.BlockSpec((1, 1, 1, 2 * D_EXPERT), bmap),
            pl.BlockSpec(memory_space=pl.ANY),
            pl.BlockSpec((1, 1, 1, D_MODEL), bmap),
        ],
        out_specs=pl.BlockSpec((MOE_M * ROW_TILE, LANES), rmap),
        scratch_shapes=[
            pltpu.VMEM((2, D_MODEL, 2 * D_EXPERT), F32),
            pltpu.VMEM((2, D_EXPERT, D_MODEL), F32),
            pltpu.VMEM((D_MODEL, 2 * D_EXPERT), BF16),
            pltpu.VMEM((D_EXPERT, D_MODEL), BF16),
            pltpu.SemaphoreType.DMA((2,)),
        ],
    )
    return pl.pallas_call(
        functools.partial(_expert_kernel, layer),
        grid_spec=grid_spec,
        out_shape=jax.ShapeDtypeStruct((N_EXPERTS * EXPERT_CAP * ROW_TILE, LANES), F32),
        compiler_params=pltpu.CompilerParams(
            dimension_semantics=("arbitrary",), vmem_limit_bytes=VMEM_LIMIT),
        name="experts",
    )(*tables, xs, w1, b1.reshape(DEPTH, N_EXPERTS, 1, 2 * D_EXPERT), w2,
      b2.reshape(DEPTH, N_EXPERTS, 1, D_MODEL))


def _block_tables(counts):
    ids = jnp.arange(N_EXPERTS, dtype=I32)
    nblk = (counts + MOE_M - 1) // MOE_M
    blk_end = jnp.cumsum(nblk)
    blk_start = blk_end - nblk
    n_used = blk_end[-1]
    step = jnp.minimum(jnp.arange(N_BLOCKS, dtype=I32), n_used - 1)
    block_exp = jnp.minimum(
        jnp.sum((step[:, None] >= blk_end[None, :]).astype(I32), axis=1), N_EXPERTS - 1)
    is_exp = block_exp[:, None] == ids[None, :]
    per_step = lambda v: jnp.sum(jnp.where(is_exp, v[None, :], 0), axis=1).astype(I32)
    block_row = block_exp * (EXPERT_CAP // MOE_M) + (step - per_step(blk_start))
    has_rows = nblk > 0
    slot = (jnp.cumsum(has_rows.astype(I32)) - 1) % 2
    later = jnp.where(has_rows[None, :] & (ids[None, :] > ids[:, None]), ids[None, :], N_EXPERTS)
    nxt = jnp.min(later, axis=1)
    nxt = jnp.where(nxt == N_EXPERTS, -1, nxt)
    return (block_exp.astype(I32), block_row.astype(I32), per_step(slot), per_step(nxt),
            n_used.astype(I32).reshape(1))


def _gather_combine(tile, idx0_ref, idx1_ref, idxn_ref, ys_hbm, x1_ref, g2_ref, gate_ref, o_ref, bufs, sem,
                    then=None):
    i = pl.program_id(0)
    n = pl.num_programs(0)
    rows = tile * TOP_K
    n_buf = len(bufs)

    def wait_gather(s):
        pltpu.make_async_copy(ys_hbm.at[pl.ds(0, rows * ROW_TILE), :], bufs[s], sem.at[s]).wait()

    @pl.when(i == 0)
    def _():
        for s, idx_ref in enumerate((idx0_ref, idx1_ref)):
            def body(r, c, s=s, idx_ref=idx_ref):
                for u in range(DMA_UNROLL):
                    j = r * DMA_UNROLL + u
                    _tile_copy(ys_hbm, idx_ref[0, 0, j], bufs[s], j, sem.at[s]).start(priority=u % 2)
                return c

            lax.fori_loop(0, rows // DMA_UNROLL, body, 0)

    def step(cur, nxt):
        wait_gather(cur)
        for j in range(rows):
            _tile_copy(ys_hbm, idxn_ref[0, 0, j], bufs[nxt], j, sem.at[nxt]).start(priority=j % 2)
        gates = gate_ref[...]
        g2 = g2_ref[0]
        for c in range(ROW_TILE):
            cs = slice(c * LANES, (c + 1) * LANES)
            acc = jnp.zeros((tile, LANES), F32)
            for kk in range(TOP_K):
                acc = acc + gates[:, kk:kk + 1] * _load_row_tile_chunk(
                    bufs[cur], kk * tile * ROW_TILE, tile, c)
            o_ref[:, cs] = x1_ref[:, cs] + g2[:, cs] * acc
        if then is not None:
            then()

        @pl.when(i == n - 1)
        def _():
            for s in range(n_buf):
                if s != cur:
                    wait_gather(s)

    for r in range(n_buf):
        @pl.when(i % n_buf == r)
        def _(r=r):
            step(r, (r + GATHER_AHEAD) % n_buf)


def _combine_kernel(idx0_ref, idx1_ref, idxn_ref, ys_hbm, x1_ref, g2_ref, gate_ref, o_ref,
                    buf0, buf1, buf2, sem):
    _gather_combine(TMC, idx0_ref, idx1_ref, idxn_ref, ys_hbm, x1_ref, g2_ref, gate_ref, o_ref,
                    (buf0, buf1, buf2), sem)


def _combine_inproj_kernel(idx0_ref, idx1_ref, idxn_ref, ys_hbm, x1_ref, g2_ref, gate_ref,
                           sh_ref, sc_ref, g_ref, w_ref, seg_ref, qg_ref, kg_ref,
                           xn_ref, ug_ref, q_ref, k_ref, v_ref, buf0, buf1, buf2, sem):
    def inproj():
        _inproj_body(xn_ref[...], sh_ref, sc_ref, g_ref, w_ref, seg_ref, qg_ref, kg_ref,
                     ug_ref, q_ref, k_ref, v_ref)

    _gather_combine(TM, idx0_ref, idx1_ref, idxn_ref, ys_hbm, x1_ref, g2_ref, gate_ref, xn_ref,
                    (buf0, buf1, buf2), sem, then=inproj)


def _choice_major(dest, tile):
    return dest.reshape(TOP_K, TOKENS // tile, tile).transpose(1, 0, 2).reshape(
        TOKENS // tile, 1, TOP_K * tile)


def _combine_inproj(dest, ys, x1, mod3_prev, gates, mod3, norm_g, w_in_bf, seg, qg, kg):
    n_tiles = TOKENS // TM
    rows = TM * TOP_K
    row = lambda i: (i, 0)
    const = lambda i: (0, 0)
    modspec = lambda j: pl.BlockSpec((1, 1, D_MODEL), lambda i: (i // TILES_PER_SEQ, 0, j))
    half_sd = jax.ShapeDtypeStruct((TOKENS, D_CONV), BF16)
    return pl.pallas_call(
        _combine_inproj_kernel,
        grid=(n_tiles,),
        in_specs=[
            pl.BlockSpec((1, 1, rows), lambda i: (0, 0, 0), memory_space=pltpu.SMEM),
            pl.BlockSpec((1, 1, rows), lambda i: (1, 0, 0), memory_space=pltpu.SMEM),
            pl.BlockSpec((1, 1, rows), lambda i: (jnp.minimum(i + GATHER_AHEAD, n_tiles - 1), 0, 0),
                         memory_space=pltpu.SMEM),
            pl.BlockSpec(memory_space=pl.ANY),
            pl.BlockSpec((TM, D_MODEL), row),
            modspec(5),
            pl.BlockSpec((TM, TOP_K), row),
            modspec(0),
            modspec(1),
            pl.BlockSpec((1, D_MODEL), const),
            pl.BlockSpec((D_MODEL, D_IN), const),
            pl.BlockSpec((D_ATTN, D_ATTN), const),
            pl.BlockSpec((1, D_ATTN), const),
            pl.BlockSpec((1, D_ATTN), const),
        ],
        out_specs=[pl.BlockSpec((TM, D_MODEL), row)] + [pl.BlockSpec((TM, D_CONV), row)] * 4,
        out_shape=[jax.ShapeDtypeStruct((TOKENS, D_MODEL), F32)] + [half_sd] * 4,
        scratch_shapes=[
            pltpu.VMEM((rows * ROW_TILE, LANES), F32),
            pltpu.VMEM((rows * ROW_TILE, LANES), F32),
            pltpu.VMEM((rows * ROW_TILE, LANES), F32),
            pltpu.SemaphoreType.DMA((GATHER_AHEAD + 1,)),
        ],
        compiler_params=pltpu.CompilerParams(
            dimension_semantics=("arbitrary",), vmem_limit_bytes=VMEM_LIMIT),
        name="combine_inproj",
    )(*([_choice_major(dest, TM)] * 3), ys, x1, mod3_prev, gates,
      mod3, mod3, norm_g, w_in_bf, seg, qg, kg)


def _combine(dest3, ys, x1, mod3, gates):
    n_tiles = TOKENS // TMC
    tiles_per_seq = SEQ // TMC
    rows = TMC * TOP_K
    return pl.pallas_call(
        _combine_kernel,
        grid=(n_tiles,),
        in_specs=[
            pl.BlockSpec((1, 1, rows), lambda i: (0, 0, 0), memory_space=pltpu.SMEM),
            pl.BlockSpec((1, 1, rows), lambda i: (1, 0, 0), memory_space=pltpu.SMEM),
            pl.BlockSpec((1, 1, rows), lambda i: (jnp.minimum(i + GATHER_AHEAD, n_tiles - 1), 0, 0),
                         memory_space=pltpu.SMEM),
            pl.BlockSpec(memory_space=pl.ANY),
            pl.BlockSpec((TMC, D_MODEL), lambda i: (i, 0)),
            pl.BlockSpec((1, 1, D_MODEL), lambda i: (i // tiles_per_seq, 0, 5)),
            pl.BlockSpec((TMC, TOP_K), lambda i: (i, 0)),
        ],
        out_specs=pl.BlockSpec((TMC, D_MODEL), lambda i: (i, 0)),
        out_shape=jax.ShapeDtypeStruct((TOKENS, D_MODEL), F32),
        scratch_shapes=[
            pltpu.VMEM((rows * ROW_TILE, LANES), F32),
            pltpu.VMEM((rows * ROW_TILE, LANES), F32),
            pltpu.VMEM((rows * ROW_TILE, LANES), F32),
            pltpu.SemaphoreType.DMA((GATHER_AHEAD + 1,)),
        ],
        compiler_params=pltpu.CompilerParams(
            dimension_semantics=("arbitrary",), vmem_limit_bytes=VMEM_LIMIT),
        name="moe_combine",
    )(dest3, dest3, dest3, ys, x1, mod3, gates)


_I = np.arange(Q_PAIR)[:, None]
_K = np.arange(KEY_WIN)[None, :]
_BAND_OK = (_K // CHUNK >= _I // CHUNK) & (_K // CHUNK <= _I // CHUNK + LEFT_CHUNKS)
_SEG = np.kron(np.eye(N_HEADS), np.full((HEAD_DIM, HEAD_DIM), 1.0 / HEAD_DIM)).astype(np.float32)
_TRI = (np.arange(TM)[:, None] < np.arange(TM)[None, :]).astype(np.float32)


def _bias_tables(rel_bias):
    n_far = KEY_WIN - 1 + Q_PAIR - 2 * REL_CLIP
    ext = jnp.concatenate(
        [rel_bias[..., 1:], jnp.repeat(rel_bias[..., 2 * REL_CLIP:], n_far, axis=-1)], axis=-1)
    rev = ext[..., ::-1]
    period = KEY_WIN + Q_PAIR
    ring = jnp.concatenate(
        [rev[..., Q_PAIR - 1:], jnp.zeros(rev.shape[:-1] + (1,), rev.dtype), rev[..., :Q_PAIR - 1]],
        axis=-1)
    lead = rel_bias.shape[:-1]
    skew = jnp.tile(ring, Q_PAIR)[..., :Q_PAIR * (period - 1)].reshape(lead + (Q_PAIR, period - 1))
    table = jnp.where(_BAND_OK, skew[..., :KEY_WIN], NEG_INF).astype(F32)
    return table.reshape(DEPTH, N_HEADS // 2, 2, Q_PAIR, KEY_WIN).transpose(0, 1, 4, 2, 3).reshape(
        DEPTH, N_HEADS // 2, KEY_WIN, 2 * Q_PAIR)


def kernel(x, c, ada_w, ada_b, norm1_g, w_in, conv_w, conv_b, conv_ln_g, conv_ln_b, q_norm_g,
           k_norm_g, rel_bias, w_out, norm2_g, router_w, router_b, exp_w1, exp_b1, exp_w2, exp_b2):
    xf = x.reshape(TOKENS, D_MODEL)
    mod = _ada_mod(c, ada_w, ada_b)
    seg = jnp.asarray(_SEG, BF16)
    tri = jnp.asarray(_TRI, BF16)
    pad_e = ((0, 0), (0, LANES - N_EXPERTS))
    bias_tables = _bias_tables(rel_bias)
    moe = None
    for l in range(DEPTH):
        mod3 = mod[l].reshape(BATCH, 1, 6 * D_MODEL)
        qg = (jnp.tile(q_norm_g[l], N_HEADS) * (HEAD_DIM ** -0.5)).reshape(1, D_ATTN)
        kg = jnp.tile(k_norm_g[l], N_HEADS).reshape(1, D_ATTN)
        inproj_params = (mod3, norm1_g[l].reshape(1, D_MODEL), w_in[l].astype(BF16), seg, qg, kg)
        if moe is None:
            ug, q, k, v = _inproj(xf, *inproj_params)
        else:
            xf, ug, q, k, v = _combine_inproj(*moe, *inproj_params)
        y_conv = _conv(ug, conv_w[l], conv_b[l].reshape(1, D_CONV),
                       conv_ln_g[l].reshape(1, D_CONV), conv_ln_b[l].reshape(1, D_CONV))
        y_attn = _attention(q, k, v, bias_tables[l])
        w_out_bf = w_out[l].astype(BF16)
        rw_hi = router_w[l].astype(BF16)
        rw_lo = (router_w[l] - rw_hi.astype(F32)).astype(BF16)
        x1, dest, gates, cnt, xs = _outproj(
            y_conv, y_attn, xf, mod3, w_out_bf[:D_CONV], w_out_bf[D_CONV:],
            norm2_g[l].reshape(1, D_MODEL), jnp.pad(rw_hi, pad_e), jnp.pad(rw_lo, pad_e),
            jnp.pad(router_b[l].reshape(1, N_EXPERTS), pad_e), tri)
        ys = _experts(l, _block_tables(cnt[:, 0].astype(I32)), xs, exp_w1, exp_b1, exp_w2, exp_b2)
        moe = (dest, ys, x1, mod3, gates.T)
    dest, ys, x1, mod3, gates_t = moe
    xf = _combine(_choice_major(dest, TMC), ys, x1, mod3, gates_t)
    return xf.reshape(BATCH, SEQ, D_MODEL)
```
